```python
import jax, jax.numpy as jnp
from jax import lax
import numpy as np

D_MODEL = 1024
BATCH = 16
SEQ = 256
DEPTH = 4
DEC_BATCH = 8
DEC_SEQ = 4096
PAST_LEN = 256

GRID_W = 64
W_BR = 512
N_BRANCH = 3
H_A = 4
DK = 128
DV = 128
CONV_K = 3
DELTA_CHUNK = 64
G_B = 4
CH_B = W_BR // G_B
CHUNK_B = 128
G_C = 4
CH_C = W_BR // G_C
N_IN = 3 * W_BR + 4 * H_A + 6 * W_BR + N_BRANCH * D_MODEL
EPS = 1e-6

kernel_name = 'hybrid_delta_sgu_fourier_flow_step'


def rmsnorm(x, g):
    xf = x.astype(jnp.float32)
    y = xf * lax.rsqrt(jnp.mean(xf * xf, axis=-1, keepdims=True) + EPS)
    return (y * g.astype(jnp.float32)).astype(x.dtype)


def l2norm(x):
    return x * lax.rsqrt(jnp.sum(x * x, axis=-1, keepdims=True) + EPS)


def ada_mod(cvec, w, b):
    m = jnp.einsum('...d,de->...e', jax.nn.silu(cvec), w) + b
    return jnp.split(m, 3, axis=-1)


def short_conv(x, w):
    y = lax.conv_general_dilated(x, w[:, None, :].astype(x.dtype), window_strides=(1,),
                                 padding=((CONV_K // 2, CONV_K // 2),),
                                 dimension_numbers=('NWC', 'WIO', 'NWC'),
                                 feature_group_count=x.shape[-1])
    return jax.nn.silu(y)


def gated_delta_chunked(q, k, v, g, beta, s0):
    B, T, H, _ = q.shape
    n = T // DELTA_CHUNK

    def blk(t):
        t = t.reshape((B, n, DELTA_CHUNK) + t.shape[2:])
        return jnp.swapaxes(t, 2, 3)

    qc, kc, vc, gc, bc = blk(q), blk(k), blk(v), blk(g), blk(beta)
    gam = jnp.cumsum(gc, axis=-1)
    pos = jnp.arange(DELTA_CHUNK)
    strict = pos[:, None] > pos[None, :]
    incl = pos[:, None] >= pos[None, :]
    diff = gam[..., :, None] - gam[..., None, :]
    dec_strict = jnp.where(strict, jnp.exp(jnp.where(strict, diff, 0.0)), 0.0)
    dec_incl = jnp.where(incl, jnp.exp(jnp.where(incl, diff, 0.0)), 0.0)
    eye = jnp.eye(DELTA_CHUNK, dtype=q.dtype)
    m = eye + bc[..., :, None] * jnp.einsum('bnhik,bnhjk->bnhij', kc, kc) * dec_strict
    rhs = jnp.concatenate([vc * bc[..., None], kc * (bc * jnp.exp(gam))[..., None]], axis=-1)
    sol = lax.linalg.triangular_solve(m, rhs, left_side=True, lower=True, unit_diagonal=True)
    u0, w = sol[..., :DV], sol[..., DV:]
    qk = jnp.einsum('bnhik,bnhjk->bnhij', qc, kc) * dec_incl
    qg = qc * jnp.exp(gam)[..., None]
    kd = kc * jnp.exp(gam[..., -1:] - gam)[..., None]
    glast = jnp.exp(gam[..., -1])
    xs = tuple(jnp.moveaxis(t, 1, 0) for t in (u0, w, qk, qg, kd, glast))

    def step(s, inp):
        u0_, w_, qk_, qg_, kd_, gl_ = inp
        u = u0_ - jnp.einsum('bhck,bhkv->bhcv', w_, s)
        o = jnp.einsum('bhck,bhkv->bhcv', qg_, s) + jnp.einsum('bhij,bhjv->bhiv', qk_, u)
        s = gl_[..., None, None] * s + jnp.einsum('bhck,bhcv->bhkv', kd_, u)
        return s, o

    s_fin, o = lax.scan(step, s0, xs)
    o = jnp.transpose(o, (1, 0, 3, 2, 4)).reshape(B, T, H, DV)
    return o, s_fin


def delta_bidir(q, k, v, g, beta, s0_f, s0_b):
    o_f, s_f = gated_delta_chunked(q, k, v, g[:, :, 0], beta[:, :, 0], s0_f)
    rev = lambda t: jnp.flip(t, axis=1)
    o_b, s_b = gated_delta_chunked(rev(q), rev(k), rev(v), rev(g[:, :, 1]), rev(beta[:, :, 1]), s0_b)
    return o_f + rev(o_b), s_f, s_b


def chunk_sgu(u, v, g_norm, w_s, b_s):
    B, T, _ = u.shape
    n = T // CHUNK_B
    vn = rmsnorm(v, g_norm).reshape(B, n, CHUNK_B, G_B, CH_B)
    vs = jnp.einsum('gpq,bnqgc->bnpgc', w_s, vn) + b_s.T[None, None, :, :, None]
    return u * vs.reshape(B, T, W_BR)


def fourier_mix(xc, grid):
    B, T, _ = xc.shape
    xf = xc.astype(jnp.float32).reshape(B, T, G_C, CH_C)
    if grid is None:
        y = jnp.fft.fftn(xf, axes=(1, 3), norm='ortho').real
    else:
        rows, cols = grid
        y = jnp.fft.fftn(xf.reshape(B, rows, cols, G_C, CH_C), axes=(1, 2, 4), norm='ortho').real
    return y.reshape(B, T, W_BR).astype(xc.dtype)


def mixer(xn, w_in, conv_w, a_log, dt_bias, o_norm_g, sgu_norm_g, w_spatial, b_spatial,
          w_branch, w_out, s0_f, s0_b, grid):
    B, T, _ = xn.shape
    sizes = (3 * W_BR, 2 * H_A, 2 * H_A, W_BR, W_BR, W_BR, W_BR, W_BR, W_BR, N_BRANCH * D_MODEL)
    points = [int(p) for p in np.cumsum(sizes)[:-1]]
    proj = jnp.einsum('btd,de->bte', xn, w_in)
    qkv, beta, dec, z_a, u_b, v_b, z_b, x_c, z_c, gates = jnp.split(proj, points, axis=-1)

    qkv = short_conv(qkv, conv_w).astype(jnp.float32)
    q, k, v = jnp.split(qkv, 3, axis=-1)
    q = l2norm(q.reshape(B, T, H_A, DK)) * (DK ** -0.5)
    k = l2norm(k.reshape(B, T, H_A, DK))
    v = v.reshape(B, T, H_A, DV)
    beta = jax.nn.sigmoid(beta.astype(jnp.float32)).reshape(B, T, 2, H_A)
    g = -jnp.exp(a_log.astype(jnp.float32)) * jax.nn.softplus(
        dec.astype(jnp.float32).reshape(B, T, 2, H_A) + dt_bias.astype(jnp.float32))
    o, s_f, s_b = delta_bidir(q, k, v, g, beta, s0_f.astype(jnp.float32), s0_b.astype(jnp.float32))
    o = rmsnorm(o, o_norm_g) * jax.nn.silu(z_a.astype(jnp.float32).reshape(B, T, H_A, DV))
    o_a = o.reshape(B, T, W_BR).astype(xn.dtype)

    o_b = chunk_sgu(u_b, v_b, sgu_norm_g, w_spatial, b_spatial) * jax.nn.silu(z_b)

    o_c = fourier_mix(x_c, grid) * jax.nn.silu(z_c)

    br = jnp.stack([o_a, o_b, o_c], axis=2)
    pb = jnp.einsum('btnw,nwd->btnd', br, w_branch)
    gt = jax.nn.sigmoid(gates.reshape(B, T, N_BRANCH, D_MODEL))
    merged = jnp.einsum('btnd,btnd->btd', gt, pb)
    return jnp.einsum('btd,de->bte', merged, w_out), s_f, s_b


def setup_inputs(seed: int = 0) -> dict:
    key = jax.random.key(seed)
    ks = jax.random.split(key, 24)
    nrm = lambda k, s: jax.random.normal(k, s, jnp.float32)
    dt = jnp.exp(jax.random.uniform(ks[8], (DEPTH, 2, H_A), jnp.float32) * (np.log(0.1) - np.log(0.001)) + np.log(0.001))
    return {
        'x_prompt': nrm(ks[0], (BATCH, SEQ, D_MODEL)),
        'x_sample': nrm(ks[1], (DEC_BATCH, DEC_SEQ, D_MODEL)),
        'state_delta': 0.5 * nrm(ks[2], (DEC_BATCH, DEPTH, 2, H_A, DK, DV)),
        'c': nrm(ks[3], (DEC_BATCH, D_MODEL)),
        'c_ctx': nrm(ks[4], (D_MODEL,)),
        'ln_g': 1.0 + 0.05 * nrm(ks[5], (DEPTH, D_MODEL)),
        'w_ada': 0.5 * D_MODEL ** -0.5 * nrm(ks[6], (DEPTH, D_MODEL, 3 * D_MODEL)),
        'b_ada': 0.02 * nrm(ks[7], (DEPTH, 3 * D_MODEL)),
        'w_in': D_MODEL ** -0.5 * nrm(ks[9], (DEPTH, D_MODEL, N_IN)),
        'conv_w': CONV_K ** -0.5 * nrm(ks[10], (DEPTH, CONV_K, 3 * W_BR)),
        'a_log': jnp.log(jax.random.uniform(ks[11], (DEPTH, 2, H_A), jnp.float32, 1.0, 16.0)),
        'dt_bias': dt + jnp.log(-jnp.expm1(-dt)),
        'o_norm_g': 1.0 + 0.05 * nrm(ks[12], (DEPTH, DV)),
        'sgu_norm_g': 1.0 + 0.05 * nrm(ks[13], (DEPTH, W_BR)),
        'w_spatial': CHUNK_B ** -0.5 * nrm(ks[14], (DEPTH, G_B, CHUNK_B, CHUNK_B)),
        'b_spatial': 1.0 + 0.02 * nrm(ks[15], (DEPTH, G_B, CHUNK_B)),
        'w_branch': W_BR ** -0.5 * nrm(ks[16], (DEPTH, N_BRANCH, W_BR, D_MODEL)),
        'w_out': D_MODEL ** -0.5 * nrm(ks[17], (DEPTH, D_MODEL, D_MODEL)),
        'final_g': 1.0 + 0.05 * nrm(ks[18], (D_MODEL,)),
    }


def reference(x_prompt, x_sample, state_delta, c, c_ctx, ln_g, w_ada, b_ada, w_in, conv_w, a_log,
              dt_bias, o_norm_g, sgu_norm_g, w_spatial, b_spatial, w_branch, w_out, final_g):
    rows = x_sample.shape[1] // GRID_W

    def lp(l):
        return (w_in[l], conv_w[l], a_log[l], dt_bias[l], o_norm_g[l], sgu_norm_g[l],
                w_spatial[l], b_spatial[l], w_branch[l], w_out[l])

    h = x_prompt
    zeros = jnp.zeros((x_prompt.shape[0], H_A, DK, DV), jnp.float32)
    states = []
    for l in range(DEPTH):
        shift, scale, gate = ada_mod(c_ctx, w_ada[l], b_ada[l])
        xn = rmsnorm(h, ln_g[l]) * (1.0 + scale) + shift
        out, s_f, s_b = mixer(xn, *lp(l), zeros, zeros, None)
        h = h + gate * out
        states.append(jnp.stack([s_f, s_b], axis=1))
    y_prompt = rmsnorm(h, final_g)
    new_state_delta = jnp.stack(states, axis=1).astype(x_prompt.dtype)

    h = x_sample
    for l in range(DEPTH):
        shift, scale, gate = ada_mod(c, w_ada[l], b_ada[l])
        xn = rmsnorm(h, ln_g[l]) * (1.0 + scale[:, None, :]) + shift[:, None, :]
        out, _, _ = mixer(xn, *lp(l), state_delta[:, l, 0], state_delta[:, l, 1], (rows, GRID_W))
        h = h + gate[:, None, :] * out
    y_sample = rmsnorm(h, final_g)
    return (y_prompt, y_sample, new_state_delta)
```

```python
import functools

import numpy as np
import jax
import jax.numpy as jnp
from jax import lax
from jax.experimental import pallas as pl
from jax.experimental.pallas import tpu as pltpu

F32 = jnp.float32
BF16 = jnp.bfloat16

W_BR = 512
H_A = 4
DK = 128
CONV_K = 3
CHUNK = 64
G_B = 4
CHUNK_B = 128
G_C = 4
CH_C = W_BR // G_C
GRID_W = 64
EPS = 1e-6
TM = 256
HALO = 8
VMEM_LIMIT = 56 * 1024 * 1024


def _bdot(a, b):
    return jnp.dot(a.astype(BF16), b.astype(BF16), preferred_element_type=F32)


def _split(x):
    hi = x.astype(BF16)
    lo = (x - hi.astype(F32)).astype(BF16)
    return hi, lo


def _dot3(x, y):
    xh, xl = _split(x)
    yh, yl = _split(y)
    d = functools.partial(jnp.dot, preferred_element_type=F32)
    return d(xh, yh) + (d(xh, yl) + d(xl, yh))


def _silu(x):
    return x * jax.nn.sigmoid(x)


def _cparams(sem):
    return pltpu.CompilerParams(dimension_semantics=sem, vmem_limit_bytes=VMEM_LIMIT)


def _ada_kernel(c_ref, w_ref, b_ref, o_ref):
    o_ref[...] = _bdot(_silu(c_ref[...]), w_ref[...]) + b_ref[...]


def _ada_call(cvecs, w_ada, b_ada):
    depth, d, e = w_ada.shape
    r = cvecs.shape[0]
    tn = 1024
    return pl.pallas_call(
        _ada_kernel,
        out_shape=jax.ShapeDtypeStruct((depth, r, e), F32),
        grid=(depth, e // tn),
        in_specs=[pl.BlockSpec((r, d), lambda l, n: (0, 0)),
                  pl.BlockSpec((None, d, tn), lambda l, n: (l, 0, n)),
                  pl.BlockSpec((None, 1, tn), lambda l, n: (l, 0, n))],
        out_specs=pl.BlockSpec((None, r, tn), lambda l, n: (l, 0, n)),
        compiler_params=_cparams(("arbitrary", "arbitrary")),
        name="ada_mod",
    )(cvecs, w_ada, b_ada.reshape(depth, 1, e))


def _in_kernel(x_ref, xp_ref, xn_ref, shift_ref, scale_ref, lng_ref, wqkv_ref, wbd_ref, wrest_ref,
               convw_ref, adt_ref, sgug_ref, wsp_ref, bsp_ref, cch_ref, mc_ref, ltri_ref, utri_ref,
               q_ref, k_ref, v_ref, bd_ref, sza_ref, osgu_ref, re_ref, im_ref, szc_ref, sg_ref,
               *, seq_tiles, grid_cols):
    i = pl.program_id(0)
    t_in_seq = i % seq_tiles
    lng = lng_ref[...]
    scale1 = 1.0 + scale_ref[...]
    shift = shift_ref[...]

    def normmod(x):
        y = x * lax.rsqrt(jnp.mean(x * x, axis=-1, keepdims=True) + EPS)
        return ((y * lng) * scale1 + shift).astype(BF16)

    xn = normmod(x_ref[...])

    wqkv = wqkv_ref[...]
    p = jnp.dot(xn, wqkv, preferred_element_type=F32)
    p_prev = jnp.dot(normmod(xp_ref[...]), wqkv, preferred_element_type=F32)[HALO - 1:HALO]
    p_next = jnp.dot(normmod(xn_ref[...]), wqkv, preferred_element_type=F32)[0:1]
    p_prev = jnp.where(t_in_seq == 0, 0.0, p_prev)
    p_next = jnp.where(t_in_seq == seq_tiles - 1, 0.0, p_next)
    row = lax.broadcasted_iota(jnp.int32, (TM, 1), 0)
    down = jnp.where(row == 0, p_prev, pltpu.roll(p, 1, axis=0))
    up = jnp.where(row == TM - 1, p_next, pltpu.roll(p, TM - 1, axis=0))
    cw = convw_ref[...]
    y = _silu(cw[0:1] * down + cw[1:2] * p + cw[2:3] * up)
    for h in range(H_A):
        qh = y[:, h * DK:(h + 1) * DK]
        kh = y[:, W_BR + h * DK:W_BR + (h + 1) * DK]
        q_ref[h] = qh * lax.rsqrt(jnp.sum(qh * qh, axis=-1, keepdims=True) + EPS) * (DK ** -0.5)
        k_ref[h] = kh * lax.rsqrt(jnp.sum(kh * kh, axis=-1, keepdims=True) + EPS)
        v_ref[h] = y[:, 2 * W_BR + h * DK:2 * W_BR + (h + 1) * DK]

    pbd = jnp.dot(xn, wbd_ref[...], preferred_element_type=F32)
    adt = adt_ref[...]
    sp_in = pbd + adt[1:2]
    softplus = jnp.maximum(sp_in, 0.0) + jnp.log1p(jnp.exp(-jnp.abs(sp_in)))
    g = -jnp.exp(adt[0:1]) * softplus
    gpre = jnp.dot(ltri_ref[...], g, preferred_element_type=F32, precision=lax.Precision.HIGHEST)
    gsuf = jnp.dot(utri_ref[...], g, preferred_element_type=F32, precision=lax.Precision.HIGHEST)
    lane = lax.broadcasted_iota(jnp.int32, (TM, 128), 1)
    bd_ref[...] = jnp.where(lane < 2 * H_A, jax.nn.sigmoid(pbd),
                            jnp.where(lane < 3 * H_A, gpre, gsuf))

    def rest(j):
        return jnp.dot(xn, wrest_ref[:, j * W_BR:(j + 1) * W_BR], preferred_element_type=F32)

    sza_ref[...] = _silu(rest(0))

    u_b, v_b, z_b = rest(1), rest(2), rest(3)
    vn = (v_b * lax.rsqrt(jnp.mean(v_b * v_b, axis=-1, keepdims=True) + EPS) * sgug_ref[...]).astype(BF16)
    gate_b = _silu(z_b)
    for n in range(TM // CHUNK_B):
        rs = slice(n * CHUNK_B, (n + 1) * CHUNK_B)
        for gi in range(G_B):
            cs = slice(gi * (W_BR // G_B), (gi + 1) * (W_BR // G_B))
            vs = jnp.dot(wsp_ref[gi], vn[rs, cs], preferred_element_type=F32) + bsp_ref[gi]
            osgu_ref[rs, cs] = ((u_b[rs, cs] * vs) * gate_b[rs, cs]).astype(BF16)

    x_c = rest(4).astype(BF16)
    szc_ref[...] = _silu(rest(5))
    cch = cch_ref[...]
    a_parts, b_parts = [], []
    for gi in range(G_C):
        ab = jnp.dot(x_c[:, gi * CH_C:(gi + 1) * CH_C], cch, preferred_element_type=F32)
        a_parts.append(ab[:, :CH_C])
        b_parts.append(ab[:, CH_C:])
    a = jnp.concatenate(a_parts, axis=1)
    b = jnp.concatenate(b_parts, axis=1)
    if grid_cols:
        mc = mc_ref[...]
        for r in range(TM // GRID_W):
            rs = slice(r * GRID_W, (r + 1) * GRID_W)
            stacked = jnp.concatenate([a[rs], b[rs]], axis=0).astype(BF16)
            z = jnp.dot(mc, stacked, preferred_element_type=F32)
            re_ref[rs, :] = z[:GRID_W].astype(BF16)
            im_ref[rs, :] = z[GRID_W:].astype(BF16)
    else:
        re_ref[...] = a.astype(BF16)
        im_ref[...] = (-b).astype(BF16)

    for n in range(3):
        for j in range(2):
            c0 = 6 * W_BR + (2 * n + j) * W_BR
            sg_ref[n, :, j * W_BR:(j + 1) * W_BR] = jax.nn.sigmoid(
                jnp.dot(xn, wrest_ref[:, c0:c0 + W_BR], preferred_element_type=F32))


def _in_call(h, shift, scale, row_of_tile, seq_tiles, grid_cols, lw, consts):
    ntok, d = h.shape
    nt = ntok // TM
    nhb = ntok // HALO
    full = lambda a: pl.BlockSpec(a.shape, lambda i: (0,) * a.ndim)
    tile = lambda w: pl.BlockSpec((TM, w), lambda i: (i, 0))
    mod = pl.BlockSpec((None, 1, d), lambda i: (row_of_tile(i), 0, 0))
    ins = [h, h, h, shift, scale, lw["ln_g"], lw["w_qkv"], lw["w_bd"], lw["w_rest"], lw["conv_w"], lw["adt"],
           lw["sgu_g"], lw["w_sp"], lw["b_sp"], consts["cch"], consts["mc"], consts["ltri"], consts["utri"]]
    in_specs = [tile(d),
                pl.BlockSpec((HALO, d), lambda i: (jnp.maximum(i * (TM // HALO) - 1, 0), 0)),
                pl.BlockSpec((HALO, d), lambda i: (jnp.minimum((i + 1) * (TM // HALO), nhb - 1), 0)),
                mod, mod] + [full(a) for a in ins[5:]]
    hsplit = pl.BlockSpec((H_A, TM, DK), lambda i: (0, i, 0))
    out_shape = [jax.ShapeDtypeStruct((H_A, ntok, DK), F32)] * 3 + [
        jax.ShapeDtypeStruct((ntok, 128), F32),
        jax.ShapeDtypeStruct((ntok, W_BR), F32),
        jax.ShapeDtypeStruct((ntok, W_BR), BF16),
        jax.ShapeDtypeStruct((ntok, W_BR), BF16),
        jax.ShapeDtypeStruct((ntok, W_BR), BF16),
        jax.ShapeDtypeStruct((ntok, W_BR), F32),
        jax.ShapeDtypeStruct((3, ntok, 2 * W_BR), F32),
    ]
    out_specs = [hsplit] * 3 + [tile(128), tile(W_BR), tile(W_BR), tile(W_BR), tile(W_BR), tile(W_BR),
                                pl.BlockSpec((3, TM, 2 * W_BR), lambda i: (0, i, 0))]
    return pl.pallas_call(
        functools.partial(_in_kernel, seq_tiles=seq_tiles, grid_cols=grid_cols),
        out_shape=out_shape, grid=(nt,), in_specs=in_specs, out_specs=out_specs,
        compiler_params=_cparams(("arbitrary",)), name="in_proj",
    )(*ins)


def _delta_kernel(qf_ref, kf_ref, vf_ref, bdf_ref, bdtf_ref, qb_ref, kb_ref, vb_ref, bdb_ref, bdtb_ref, s0_ref,
                  of_ref, ob_ref, s_ref):
    j = pl.program_id(1)

    @pl.when(j == 0)
    def _():
        s_ref[...] = s0_ref[...]

    nch = TM // CHUNK
    ri = lax.broadcasted_iota(jnp.int32, (CHUNK, CHUNK), 0)
    ci = lax.broadcasted_iota(jnp.int32, (CHUNK, CHUNK), 1)
    eye = (ri == ci).astype(F32)
    masks = (ri > ci, ri < ci)
    refs = ((qf_ref, kf_ref, vf_ref, bdf_ref, bdtf_ref, of_ref), (qb_ref, kb_ref, vb_ref, bdb_ref, bdtb_ref, ob_ref))

    def step(s, carry):
        for d in range(2):
            q_ref, k_ref, v_ref, bd_ref, bdt_ref, o_ref = refs[d]
            c = s if d == 0 else nch - 1 - s
            sl = pl.ds(pl.multiple_of(c * CHUNK, CHUNK), CHUNK)
            mask = masks[d]
            for h in range(H_A):
                col = d * H_A + h
                q, k, v = q_ref[h, sl, :], k_ref[h, sl, :], v_ref[h, sl, :]
                beta = bd_ref[sl, col:col + 1]
                gam = bd_ref[sl, 2 * H_A + col:2 * H_A + col + 1]
                gam_row = bdt_ref[c, 2 * H_A + col:2 * H_A + col + 1, :]
                tot = gam_row[:, CHUNK - 1:CHUNK] if d == 0 else gam_row[:, 0:1]
                dec = jnp.where(mask, jnp.exp(jnp.where(mask, gam - gam_row, 0.0)), 0.0)
                kb = k.astype(BF16)
                qkk = lax.dot_general(jnp.concatenate([q.astype(BF16), kb], axis=0), kb,
                                      (((1,), (1,)), ((), ())), preferred_element_type=F32)
                pw = -((beta * qkk[CHUNK:]) * dec)
                t = eye + pw
                for _ in range(5):
                    pw = _dot3(pw, pw)
                    t = t + _dot3(t, pw)
                eg = jnp.exp(gam)
                rhs = jnp.concatenate([v * beta, k * (beta * eg)], axis=1)
                sol = _dot3(t, rhs)
                u0, w = sol[:, :DK], sol[:, DK:]
                qk = qkk[:CHUNK] * (dec + eye)
                st = s_ref[d, h]
                stb = st.astype(BF16)
                wq = jnp.dot(jnp.concatenate([w.astype(BF16), (q * eg).astype(BF16)], axis=0), stb,
                             preferred_element_type=F32)
                u = u0 - wq[:CHUNK]
                ub = u.astype(BF16)
                o_ref[sl, h * DK:(h + 1) * DK] = wq[CHUNK:] + jnp.dot(qk.astype(BF16), ub,
                                                                      preferred_element_type=F32)
                kd = (k * jnp.exp(tot - gam)).astype(BF16)
                s_ref[d, h] = jnp.exp(tot) * st + lax.dot_general(kd, ub, (((0,), (0,)), ((), ())),
                                                                  preferred_element_type=F32)
        return carry

    lax.fori_loop(0, nch, step, 0)


def _delta_call(q, k, v, bd, bdt, s0, nb):
    ntok = bd.shape[0]
    bsz = ntok // (nb * TM)
    nch = TM // CHUNK
    fw = lambda b, j: b * nb + j
    bw = lambda b, j: b * nb + nb - 1 - j

    def specs(t):
        hs = pl.BlockSpec((H_A, TM, DK), lambda b, j: (0, t(b, j), 0))
        return [hs, hs, hs, pl.BlockSpec((TM, 128), lambda b, j: (t(b, j), 0)),
                pl.BlockSpec((nch, 16, CHUNK), lambda b, j: (t(b, j), 0, 0))]

    sspec = pl.BlockSpec((None, 2, H_A, DK, DK), lambda b, j: (b, 0, 0, 0, 0))
    return pl.pallas_call(
        _delta_kernel,
        out_shape=[jax.ShapeDtypeStruct((ntok, W_BR), F32), jax.ShapeDtypeStruct((ntok, W_BR), F32),
                   jax.ShapeDtypeStruct((bsz, 2, H_A, DK, DK), F32)],
        grid=(bsz, nb),
        in_specs=specs(fw) + specs(bw) + [sspec],
        out_specs=[pl.BlockSpec((TM, W_BR), lambda b, j: (fw(b, j), 0)),
                   pl.BlockSpec((TM, W_BR), lambda b, j: (bw(b, j), 0)), sspec],
        compiler_params=_cparams(("arbitrary", "arbitrary")), name="delta_scan",
    )(q, k, v, bd, bdt, q, k, v, bd, bdt, s0)


def _fourier_kernel(cr_ref, sr_ref, re_ref, im_ref, szc_ref, o_ref):
    y = (jnp.dot(cr_ref[...], re_ref[...], preferred_element_type=F32)
         + jnp.dot(sr_ref[...], im_ref[...], preferred_element_type=F32))
    o_ref[...] = (y * szc_ref[...]).astype(BF16)


def _fourier_call(cr, sr, re, im, szc, tn):
    bsz, r, n = re.shape
    blk = pl.BlockSpec((None, r, tn), lambda b, c: (b, 0, c))
    mat = pl.BlockSpec((r, r), lambda b, c: (0, 0))
    return pl.pallas_call(
        _fourier_kernel, out_shape=jax.ShapeDtypeStruct((bsz, r, n), BF16),
        grid=(bsz, n // tn), in_specs=[mat, mat, blk, blk, blk], out_specs=blk,
        compiler_params=_cparams(("arbitrary", "arbitrary")), name="fourier_rows",
    )(cr, sr, re, im, szc)


def _merge_kernel(h_ref, of_ref, ob_ref, sza_ref, osgu_ref, oc_ref, sg_ref, gate_ref, ong_ref, wbr_ref, wout_ref,
                  fing_ref, *out_refs, final):
    o = of_ref[...] + ob_ref[...]
    sza = sza_ref[...]
    ong = ong_ref[...]
    parts = []
    for hh in range(H_A):
        oh = o[:, hh * DK:(hh + 1) * DK]
        oh = (oh * lax.rsqrt(jnp.mean(oh * oh, axis=-1, keepdims=True) + EPS)) * ong
        parts.append((oh * sza[:, hh * DK:(hh + 1) * DK]).astype(BF16))
    o_a = jnp.concatenate(parts, axis=1)
    merged = (sg_ref[0] * jnp.dot(o_a, wbr_ref[0], preferred_element_type=F32)
              + sg_ref[1] * jnp.dot(osgu_ref[...], wbr_ref[1], preferred_element_type=F32)
              + sg_ref[2] * jnp.dot(oc_ref[...], wbr_ref[2], preferred_element_type=F32))
    out = jnp.dot(merged.astype(BF16), wout_ref[...], preferred_element_type=F32)
    hn = h_ref[...] + gate_ref[...] * out
    if final:
        out_refs[0][...] = (hn * lax.rsqrt(jnp.mean(hn * hn, axis=-1, keepdims=True) + EPS)) * fing_ref[...]
    else:
        out_refs[0][...] = hn


def _merge_call(h, o_f, o_b, sza, osgu, oc, sg, gate, row_of_tile, lw, final_g, final):
    ntok, d = h.shape
    full = lambda a: pl.BlockSpec(a.shape, lambda i: (0,) * a.ndim)
    tile = lambda w: pl.BlockSpec((TM, w), lambda i: (i, 0))
    ins = [h, o_f, o_b, sza, osgu, oc, sg, gate, lw["o_norm_g"], lw["w_branch"], lw["w_out"], final_g]
    in_specs = [tile(d), tile(W_BR), tile(W_BR), tile(W_BR), tile(W_BR), tile(W_BR),
                pl.BlockSpec((3, TM, d), lambda i: (0, i, 0)),
                pl.BlockSpec((None, 1, d), lambda i: (row_of_tile(i), 0, 0))] + [full(a) for a in ins[8:]]
    return pl.pallas_call(
        functools.partial(_merge_kernel, final=final),
        out_shape=jax.ShapeDtypeStruct((ntok, d), F32), grid=(ntok // TM,),
        in_specs=in_specs, out_specs=tile(d),
        compiler_params=_cparams(("arbitrary",)), name="merge_out",
    )(*ins)


def _dft(n):
    idx = np.arange(n)
    ang = 2.0 * np.pi * ((idx[:, None] * idx[None, :]) % n) / n
    return np.cos(ang) / np.sqrt(n), np.sin(ang) / np.sqrt(n)


def _consts(seq, rows):
    cc, sc = _dft(CH_C)
    cg, sg = _dft(GRID_W)
    cp, sp = _dft(seq)
    cr, sr = _dft(rows)
    blk = np.kron(np.eye(TM // CHUNK), np.tril(np.ones((CHUNK, CHUNK))))
    bf = lambda a: jnp.asarray(a, F32).astype(BF16)
    return {
        "cch": bf(np.concatenate([cc, sc], axis=1)),
        "mc": bf(np.block([[cg, -sg], [-sg, -cg]])),
        "ltri": jnp.asarray(blk, F32), "utri": jnp.asarray(blk.T, F32),
        "cp": bf(cp), "sp": bf(sp), "cr": bf(cr), "sr": bf(sr),
    }


def _layer_weights(l, w_in, ln_g, conv_w, a_log, dt_bias, o_norm_g, sgu_norm_g, w_spatial, b_spatial, w_branch, w_out):
    d = w_in.shape[1]
    wl = w_in[l]
    nq = 3 * W_BR
    pad = jnp.zeros((2, 128 - 4 * H_A), F32)
    adt = jnp.concatenate([jnp.zeros((2, 2 * H_A), F32),
                           jnp.stack([a_log[l].reshape(-1), dt_bias[l].reshape(-1)]), pad], axis=1)
    return {
        "ln_g": ln_g[l].reshape(1, d),
        "w_qkv": wl[:, :nq].astype(BF16),
        "w_bd": jnp.concatenate([wl[:, nq:nq + 4 * H_A], jnp.zeros((d, 128 - 4 * H_A), F32)], axis=1).astype(BF16),
        "w_rest": wl[:, nq + 4 * H_A:].astype(BF16),
        "conv_w": conv_w[l],
        "adt": adt,
        "sgu_g": sgu_norm_g[l].reshape(1, W_BR),
        "w_sp": w_spatial[l].astype(BF16),
        "b_sp": jnp.broadcast_to(b_spatial[l][:, :, None], (G_B, CHUNK_B, W_BR // G_B)),
        "o_norm_g": o_norm_g[l].reshape(1, DK),
        "w_branch": w_branch[l].astype(BF16),
        "w_out": w_out[l].astype(BF16),
    }


def _group_forward(x, mods, row_of_tile, s0_all, grid_cols, weights, consts, final_g):
    bsz, t, d = x.shape
    ntok = bsz * t
    nb = t // TM
    h = x.reshape(ntok, d)
    states = []
    depth = len(weights)
    for l in range(depth):
        lw = weights[l]
        shift, scale, gate = mods[l, 0], mods[l, 1], mods[l, 2]
        q, k, v, bd, sza, osgu, re, im, szc, sg = _in_call(h, shift, scale, row_of_tile, nb, grid_cols, lw, consts)
        bdt = jnp.swapaxes(bd[:, :16].reshape(ntok // CHUNK, CHUNK, 16), 1, 2)
        o_f, o_b, s_fin = _delta_call(q, k, v, bd, bdt, s0_all[:, l], nb)
        states.append(s_fin)
        if grid_cols:
            rows = t // GRID_W
            shp = (bsz, rows, GRID_W * W_BR)
            oc = _fourier_call(consts["cr"], consts["sr"], re.reshape(shp), im.reshape(shp), szc.reshape(shp), 4096)
        else:
            shp = (bsz, t, W_BR)
            oc = _fourier_call(consts["cp"], consts["sp"], re.reshape(shp), im.reshape(shp), szc.reshape(shp), W_BR)
        h = _merge_call(h, o_f, o_b, sza, osgu, oc.reshape(ntok, W_BR), sg, gate, row_of_tile, lw,
                        final_g.reshape(1, d), l == depth - 1)
    return h.reshape(bsz, t, d), states


def kernel(x_prompt, x_sample, state_delta, c, c_ctx, ln_g, w_ada, b_ada, w_in, conv_w, a_log, dt_bias, o_norm_g,
           sgu_norm_g, w_spatial, b_spatial, w_branch, w_out, final_g):
    depth, d = ln_g.shape
    bp, tp, _ = x_prompt.shape
    bs, ts, _ = x_sample.shape
    consts = _consts(tp, ts // GRID_W)
    weights = [_layer_weights(l, w_in, ln_g, conv_w, a_log, dt_bias, o_norm_g, sgu_norm_g, w_spatial, b_spatial,
                              w_branch, w_out) for l in range(depth)]
    nrow = -(-(1 + bs) // 8) * 8
    cvecs = jnp.concatenate([c_ctx[None], c, jnp.zeros((nrow - 1 - bs, d), F32)], axis=0)
    mods = _ada_call(cvecs, w_ada, b_ada).reshape(depth, nrow, 3, 1, d).transpose(0, 2, 1, 3, 4)

    zeros = jnp.zeros((bp, depth, 2, H_A, DK, DK), F32)
    y_prompt, st = _group_forward(x_prompt, mods, lambda i: 0, zeros, False, weights, consts, final_g)
    tiles_s = ts // TM
    y_sample, _ = _group_forward(x_sample, mods, lambda i: 1 + i // tiles_s, state_delta, True, weights, consts,
                                 final_g)
    return y_prompt, y_sample, jnp.stack(st, axis=1)
```

```python
import functools

import numpy as np
import jax
import jax.numpy as jnp
from jax import lax
from jax.experimental import pallas as pl
from jax.experimental.pallas import tpu as pltpu

F32 = jnp.float32
BF16 = jnp.bfloat16

W_BR = 512
H_A = 4
DK = 128
CONV_K = 3
CHUNK = 64
PAIR = 2 * CHUNK
G_B = 4
CHUNK_B = 128
G_C = 4
CH_C = W_BR // G_C
GRID_W = 64
EPS = 1e-6
TM = 256
HALO = 8
VMEM_LIMIT = 56 * 1024 * 1024


def _bdot(a, b):
    return jnp.dot(a.astype(BF16), b.astype(BF16), preferred_element_type=F32)


def _split(x):
    hi = x.astype(BF16)
    lo = (x - hi.astype(F32)).astype(BF16)
    return hi, lo


def _dot3(x, y):
    xh, xl = _split(x)
    yh, yl = _split(y)
    return (jnp.dot(jnp.concatenate([xh, xl], axis=1), jnp.concatenate([yh, yh], axis=0),
                    preferred_element_type=F32)
            + jnp.dot(xh, yl, preferred_element_type=F32))


def _silu(x):
    return x * jax.nn.sigmoid(x)


def _cparams(sem):
    return pltpu.CompilerParams(dimension_semantics=sem, vmem_limit_bytes=VMEM_LIMIT)


def _ada_kernel(c_ref, w_ref, b_ref, o_ref):
    o_ref[...] = _bdot(_silu(c_ref[...]), w_ref[...]) + b_ref[...]


def _ada_call(cvecs, w_ada, b_ada):
    depth, d, e = w_ada.shape
    r = cvecs.shape[0]
    tn = 1024
    return pl.pallas_call(
        _ada_kernel,
        out_shape=jax.ShapeDtypeStruct((depth, r, e), F32),
        grid=(depth, e // tn),
        in_specs=[pl.BlockSpec((r, d), lambda l, n: (0, 0)),
                  pl.BlockSpec((None, d, tn), lambda l, n: (l, 0, n)),
                  pl.BlockSpec((None, 1, tn), lambda l, n: (l, 0, n))],
        out_specs=pl.BlockSpec((None, r, tn), lambda l, n: (l, 0, n)),
        compiler_params=_cparams(("arbitrary", "arbitrary")),
        name="ada_mod",
    )(cvecs, w_ada, b_ada.reshape(depth, 1, e))


def _in_kernel(x_ref, xp_ref, xn_ref, shift_ref, scale_ref, lng_ref, wqkv_ref, wbd_ref, wrest_ref,
               convw_ref, adt_ref, sgug_ref, wsp_ref, bsp_ref, cch_ref, mc_ref, ltri_ref, utri_ref,
               q_ref, k_ref, v_ref, bd_ref, sza_ref, osgu_ref, re_ref, im_ref, szc_ref, sg_ref,
               *, seq_tiles, grid_cols):
    i = pl.program_id(0)
    t_in_seq = i % seq_tiles
    lng = lng_ref[...]
    scale1 = 1.0 + scale_ref[...]
    shift = shift_ref[...]

    def normmod(x):
        y = x * lax.rsqrt(jnp.mean(x * x, axis=-1, keepdims=True) + EPS)
        return ((y * lng) * scale1 + shift).astype(BF16)

    xn = normmod(x_ref[...])

    wqkv = wqkv_ref[...]
    p = jnp.dot(xn, wqkv, preferred_element_type=F32)
    p_prev = jnp.dot(normmod(xp_ref[...]), wqkv, preferred_element_type=F32)[HALO - 1:HALO]
    p_next = jnp.dot(normmod(xn_ref[...]), wqkv, preferred_element_type=F32)[0:1]
    p_prev = jnp.where(t_in_seq == 0, 0.0, p_prev)
    p_next = jnp.where(t_in_seq == seq_tiles - 1, 0.0, p_next)
    row = lax.broadcasted_iota(jnp.int32, (TM, 1), 0)
    down = jnp.where(row == 0, p_prev, pltpu.roll(p, 1, axis=0))
    up = jnp.where(row == TM - 1, p_next, pltpu.roll(p, TM - 1, axis=0))
    cw = convw_ref[...]
    y = _silu(cw[0:1] * down + cw[1:2] * p + cw[2:3] * up)
    for h in range(H_A):
        qh = y[:, h * DK:(h + 1) * DK]
        kh = y[:, W_BR + h * DK:W_BR + (h + 1) * DK]
        q_ref[h] = qh * lax.rsqrt(jnp.sum(qh * qh, axis=-1, keepdims=True) + EPS) * (DK ** -0.5)
        k_ref[h] = kh * lax.rsqrt(jnp.sum(kh * kh, axis=-1, keepdims=True) + EPS)
        v_ref[h] = y[:, 2 * W_BR + h * DK:2 * W_BR + (h + 1) * DK]

    pbd = jnp.dot(xn, wbd_ref[...], preferred_element_type=F32)
    adt = adt_ref[...]
    sp_in = pbd + adt[1:2]
    softplus = jnp.maximum(sp_in, 0.0) + jnp.log1p(jnp.exp(-jnp.abs(sp_in)))
    g = -jnp.exp(adt[0:1]) * softplus
    gpre = jnp.dot(ltri_ref[...], g, preferred_element_type=F32, precision=lax.Precision.HIGHEST)
    gsuf = jnp.dot(utri_ref[...], g, preferred_element_type=F32, precision=lax.Precision.HIGHEST)
    lane = lax.broadcasted_iota(jnp.int32, (TM, 128), 1)
    bd_ref[...] = jnp.where(lane < 2 * H_A, jax.nn.sigmoid(pbd),
                            jnp.where(lane < 3 * H_A, gpre, gsuf))

    def rest(j):
        return jnp.dot(xn, wrest_ref[:, j * W_BR:(j + 1) * W_BR], preferred_element_type=F32)

    sza_ref[...] = _silu(rest(0))

    u_b, v_b, z_b = rest(1), rest(2), rest(3)
    vn = (v_b * lax.rsqrt(jnp.mean(v_b * v_b, axis=-1, keepdims=True) + EPS) * sgug_ref[...]).astype(BF16)
    gate_b = _silu(z_b)
    for n in range(TM // CHUNK_B):
        rs = slice(n * CHUNK_B, (n + 1) * CHUNK_B)
        for gi in range(G_B):
            cs = slice(gi * (W_BR // G_B), (gi + 1) * (W_BR // G_B))
            vs = jnp.dot(wsp_ref[gi], vn[rs, cs], preferred_element_type=F32) + bsp_ref[gi]
            osgu_ref[rs, cs] = ((u_b[rs, cs] * vs) * gate_b[rs, cs]).astype(BF16)

    x_c = rest(4).astype(BF16)
    szc_ref[...] = _silu(rest(5))
    cch = cch_ref[...]
    a_parts, b_parts = [], []
    for gi in range(G_C):
        ab = jnp.dot(x_c[:, gi * CH_C:(gi + 1) * CH_C], cch, preferred_element_type=F32)
        a_parts.append(ab[:, :CH_C])
        b_parts.append(ab[:, CH_C:])
    a = jnp.concatenate(a_parts, axis=1)
    b = jnp.concatenate(b_parts, axis=1)
    if grid_cols:
        mc = mc_ref[...]
        for r in range(TM // GRID_W):
            rs = slice(r * GRID_W, (r + 1) * GRID_W)
            stacked = jnp.concatenate([a[rs], b[rs]], axis=0).astype(BF16)
            z = jnp.dot(mc, stacked, preferred_element_type=F32)
            re_ref[rs, :] = z[:GRID_W].astype(BF16)
            im_ref[rs, :] = z[GRID_W:].astype(BF16)
    else:
        re_ref[...] = a.astype(BF16)
        im_ref[...] = (-b).astype(BF16)

    for n in range(3):
        for j in range(2):
            c0 = 6 * W_BR + (2 * n + j) * W_BR
            sg_ref[n, :, j * W_BR:(j + 1) * W_BR] = jax.nn.sigmoid(
                jnp.dot(xn, wrest_ref[:, c0:c0 + W_BR], preferred_element_type=F32))


def _in_call(h, shift, scale, row_of_tile, seq_tiles, grid_cols, lw, consts):
    ntok, d = h.shape
    nt = ntok // TM
    nhb = ntok // HALO
    full = lambda a: pl.BlockSpec(a.shape, lambda i: (0,) * a.ndim)
    tile = lambda w: pl.BlockSpec((TM, w), lambda i: (i, 0))
    mod = pl.BlockSpec((None, 1, d), lambda i: (row_of_tile(i), 0, 0))
    ins = [h, h, h, shift, scale, lw["ln_g"], lw["w_qkv"], lw["w_bd"], lw["w_rest"], lw["conv_w"], lw["adt"],
           lw["sgu_g"], lw["w_sp"], lw["b_sp"], consts["cch"], consts["mc"], consts["ltri"], consts["utri"]]
    in_specs = [tile(d),
                pl.BlockSpec((HALO, d), lambda i: (jnp.maximum(i * (TM // HALO) - 1, 0), 0)),
                pl.BlockSpec((HALO, d), lambda i: (jnp.minimum((i + 1) * (TM // HALO), nhb - 1), 0)),
                mod, mod] + [full(a) for a in ins[5:]]
    hsplit = pl.BlockSpec((H_A, TM, DK), lambda i: (0, i, 0))
    out_shape = [jax.ShapeDtypeStruct((H_A, ntok, DK), F32)] * 3 + [
        jax.ShapeDtypeStruct((ntok, 128), F32),
        jax.ShapeDtypeStruct((ntok, W_BR), F32),
        jax.ShapeDtypeStruct((ntok, W_BR), BF16),
        jax.ShapeDtypeStruct((ntok, W_BR), BF16),
        jax.ShapeDtypeStruct((ntok, W_BR), BF16),
        jax.ShapeDtypeStruct((ntok, W_BR), F32),
        jax.ShapeDtypeStruct((3, ntok, 2 * W_BR), F32),
    ]
    out_specs = [hsplit] * 3 + [tile(128), tile(W_BR), tile(W_BR), tile(W_BR), tile(W_BR), tile(W_BR),
                                pl.BlockSpec((3, TM, 2 * W_BR), lambda i: (0, i, 0))]
    return pl.pallas_call(
        functools.partial(_in_kernel, seq_tiles=seq_tiles, grid_cols=grid_cols),
        out_shape=out_shape, grid=(nt,), in_specs=in_specs, out_specs=out_specs,
        compiler_params=_cparams(("arbitrary",)), name="in_proj",
    )(*ins)


def _delta_kernel(qf_ref, kf_ref, vf_ref, bdf_ref, bdtf_ref, qb_ref, kb_ref, vb_ref, bdb_ref, bdtb_ref, s0_ref,
                  of_ref, ob_ref, s_ref):
    j = pl.program_id(1)

    @pl.when(j == 0)
    def _():
        s_ref[...] = s0_ref[...]

    nch = TM // CHUNK
    ri = lax.broadcasted_iota(jnp.int32, (PAIR, PAIR), 0)
    ci = lax.broadcasted_iota(jnp.int32, (PAIR, PAIR), 1)
    same = (ri < CHUNK) == (ci < CHUNK)
    eye = (ri == ci).astype(F32)
    masks = (same & (ri > ci), same & (ri < ci))
    top = lax.broadcasted_iota(jnp.int32, (PAIR, 1), 0) < CHUNK
    refs = ((qf_ref, kf_ref, vf_ref, bdf_ref, bdtf_ref, of_ref), (qb_ref, kb_ref, vb_ref, bdb_ref, bdtb_ref, ob_ref))
    tr_rhs = (((1,), (1,)), ((), ()))
    tr_lhs = (((0,), (0,)), ((), ()))

    def blockdiag(x):
        return jnp.concatenate([jnp.where(top, x, 0.0), jnp.where(top, 0.0, x)], axis=1)

    def step(s, carry):
        probs = [(d, p) for d in range(2) for p in range(H_A // 2)]
        ld = []
        for d, p in probs:
            q_ref, k_ref, v_ref, bd_ref, bdt_ref, _ = refs[d]
            c = s if d == 0 else nch - 1 - s
            sl = pl.ds(pl.multiple_of(c * CHUNK, CHUNK), CHUNK)
            pair = lambda r: jnp.concatenate([r[2 * p, sl, :], r[2 * p + 1, sl, :]], axis=0)
            col = d * H_A + 2 * p
            colpair = lambda c0: jnp.concatenate([bd_ref[sl, c0:c0 + 1], bd_ref[sl, c0 + 1:c0 + 2]], axis=0)
            gam = colpair(2 * H_A + col)
            last = CHUNK - 1 if d == 0 else 0
            tots = [gam[e * CHUNK + last:e * CHUNK + last + 1] for e in range(2)]
            ld.append(dict(q=pair(q_ref), k=pair(k_ref), v=pair(v_ref), beta=colpair(col), gam=gam,
                           gam_row=bdt_ref[c, 2 * d + p:2 * d + p + 1, :], tots=tots,
                           st=s_ref[d, 2 * p:2 * p + 2].reshape(2 * DK, DK), sl=sl))
        n = len(probs)
        qkk, dec = [], []
        for i, (d, p) in enumerate(probs):
            x = ld[i]
            kb = x["k"].astype(BF16)
            qkk.append(lax.dot_general(jnp.concatenate([x["q"].astype(BF16), kb], axis=0), kb, tr_rhs,
                                       preferred_element_type=F32))
            m = masks[d]
            dec.append(jnp.where(m, jnp.exp(jnp.where(m, x["gam"] - x["gam_row"], 0.0)), 0.0))
        pw = [-((ld[i]["beta"] * qkk[i][PAIR:]) * dec[i]) for i in range(n)]
        t = [eye + pw[i] for i in range(n)]
        pw = [_dot3(pw[i], pw[i]) for i in range(n)]
        for _ in range(4):
            tp = [_dot3(jnp.concatenate([t[i], pw[i]], axis=0), pw[i]) for i in range(n)]
            t = [t[i] + tp[i][:PAIR] for i in range(n)]
            pw = [tp[i][PAIR:] for i in range(n)]
        t = [t[i] + _dot3(t[i], pw[i]) for i in range(n)]
        eg = [jnp.exp(ld[i]["gam"]) for i in range(n)]
        sol = [_dot3(t[i], jnp.concatenate([ld[i]["v"] * ld[i]["beta"], ld[i]["k"] * (ld[i]["beta"] * eg[i])], axis=1))
               for i in range(n)]
        wq = []
        for i in range(n):
            lhs = jnp.concatenate([blockdiag(sol[i][:, DK:]), blockdiag(ld[i]["q"] * eg[i])], axis=0)
            wq.append(jnp.dot(lhs.astype(BF16), ld[i]["st"].astype(BF16), preferred_element_type=F32))
        ub = [(sol[i][:, :DK] - wq[i][:PAIR]).astype(BF16) for i in range(n)]
        outs, snew = [], []
        for i in range(n):
            x = ld[i]
            qk = (qkk[i][:PAIR] * (dec[i] + eye)).astype(BF16)
            outs.append(wq[i][PAIR:] + jnp.dot(qk, ub[i], preferred_element_type=F32))
            tot = jnp.concatenate([jnp.broadcast_to(tt, (CHUNK, 1)) for tt in x["tots"]], axis=0)
            kd = blockdiag(x["k"] * jnp.exp(tot - x["gam"])).astype(BF16)
            gl = jnp.concatenate([jnp.broadcast_to(jnp.exp(tt), (DK, 1)) for tt in x["tots"]], axis=0)
            snew.append(gl * x["st"] + lax.dot_general(kd, ub[i], tr_lhs, preferred_element_type=F32))
        for i, (d, p) in enumerate(probs):
            o_ref = refs[d][5]
            for e in range(2):
                hh = 2 * p + e
                o_ref[ld[i]["sl"], hh * DK:(hh + 1) * DK] = outs[i][e * CHUNK:(e + 1) * CHUNK]
                s_ref[d, hh] = snew[i][e * DK:(e + 1) * DK]
        return carry

    lax.fori_loop(0, nch, step, 0)


def _delta_call(q, k, v, bd, bdt, s0, nb):
    ntok = bd.shape[0]
    bsz = ntok // (nb * TM)
    nch = TM // CHUNK
    fw = lambda b, j: b * nb + j
    bw = lambda b, j: b * nb + nb - 1 - j

    def specs(t):
        hs = pl.BlockSpec((H_A, TM, DK), lambda b, j: (0, t(b, j), 0))
        return [hs, hs, hs, pl.BlockSpec((TM, 128), lambda b, j: (t(b, j), 0)),
                pl.BlockSpec((nch, H_A, PAIR), lambda b, j: (t(b, j), 0, 0))]

    sspec = pl.BlockSpec((None, 2, H_A, DK, DK), lambda b, j: (b, 0, 0, 0, 0))
    return pl.pallas_call(
        _delta_kernel,
        out_shape=[jax.ShapeDtypeStruct((ntok, W_BR), F32), jax.ShapeDtypeStruct((ntok, W_BR), F32),
                   jax.ShapeDtypeStruct((bsz, 2, H_A, DK, DK), F32)],
        grid=(bsz, nb),
        in_specs=specs(fw) + specs(bw) + [sspec],
        out_specs=[pl.BlockSpec((TM, W_BR), lambda b, j: (fw(b, j), 0)),
                   pl.BlockSpec((TM, W_BR), lambda b, j: (bw(b, j), 0)), sspec],
        compiler_params=_cparams(("arbitrary", "arbitrary")), name="delta_scan",
    )(q, k, v, bd, bdt, q, k, v, bd, bdt, s0)


def _fourier_kernel(cr_ref, sr_ref, re_ref, im_ref, szc_ref, o_ref):
    y = (jnp.dot(cr_ref[...], re_ref[...], preferred_element_type=F32)
         + jnp.dot(sr_ref[...], im_ref[...], preferred_element_type=F32))
    o_ref[...] = (y * szc_ref[...]).astype(BF16)


def _fourier_call(cr, sr, re, im, szc, tn):
    bsz, r, n = re.shape
    blk = pl.BlockSpec((None, r, tn), lambda b, c: (b, 0, c))
    mat = pl.BlockSpec((r, r), lambda b, c: (0, 0))
    return pl.pallas_call(
        _fourier_kernel, out_shape=jax.ShapeDtypeStruct((bsz, r, n), BF16),
        grid=(bsz, n // tn), in_specs=[mat, mat, blk, blk, blk], out_specs=blk,
        compiler_params=_cparams(("arbitrary", "arbitrary")), name="fourier_rows",
    )(cr, sr, re, im, szc)


def _merge_kernel(h_ref, of_ref, ob_ref, sza_ref, osgu_ref, oc_ref, sg_ref, gate_ref, ong_ref, wbr_ref, wout_ref,
                  fing_ref, *out_refs, final):
    o = of_ref[...] + ob_ref[...]
    sza = sza_ref[...]
    ong = ong_ref[...]
    parts = []
    for hh in range(H_A):
        oh = o[:, hh * DK:(hh + 1) * DK]
        oh = (oh * lax.rsqrt(jnp.mean(oh * oh, axis=-1, keepdims=True) + EPS)) * ong
        parts.append((oh * sza[:, hh * DK:(hh + 1) * DK]).astype(BF16))
    o_a = jnp.concatenate(parts, axis=1)
    merged = (sg_ref[0] * jnp.dot(o_a, wbr_ref[0], preferred_element_type=F32)
              + sg_ref[1] * jnp.dot(osgu_ref[...], wbr_ref[1], preferred_element_type=F32)
              + sg_ref[2] * jnp.dot(oc_ref[...], wbr_ref[2], preferred_element_type=F32))
    out = jnp.dot(merged.astype(BF16), wout_ref[...], preferred_element_type=F32)
    hn = h_ref[...] + gate_ref[...] * out
    if final:
        out_refs[0][...] = (hn * lax.rsqrt(jnp.mean(hn * hn, axis=-1, keepdims=True) + EPS)) * fing_ref[...]
    else:
        out_refs[0][...] = hn


def _merge_call(h, o_f, o_b, sza, osgu, oc, sg, gate, row_of_tile, lw, final_g, final):
    ntok, d = h.shape
    full = lambda a: pl.BlockSpec(a.shape, lambda i: (0,) * a.ndim)
    tile = lambda w: pl.BlockSpec((TM, w), lambda i: (i, 0))
    ins = [h, o_f, o_b, sza, osgu, oc, sg, gate, lw["o_norm_g"], lw["w_branch"], lw["w_out"], final_g]
    in_specs = [tile(d), tile(W_BR), tile(W_BR), tile(W_BR), tile(W_BR), tile(W_BR),
                pl.BlockSpec((3, TM, d), lambda i: (0, i, 0)),
                pl.BlockSpec((None, 1, d), lambda i: (row_of_tile(i), 0, 0))] + [full(a) for a in ins[8:]]
    return pl.pallas_call(
        functools.partial(_merge_kernel, final=final),
        out_shape=jax.ShapeDtypeStruct((ntok, d), F32), grid=(ntok // TM,),
        in_specs=in_specs, out_specs=tile(d),
        compiler_params=_cparams(("arbitrary",)), name="merge_out",
    )(*ins)


def _dft(n):
    idx = np.arange(n)
    ang = 2.0 * np.pi * ((idx[:, None] * idx[None, :]) % n) / n
    return np.cos(ang) / np.sqrt(n), np.sin(ang) / np.sqrt(n)


def _consts(seq, rows):
    cc, sc = _dft(CH_C)
    cg, sg = _dft(GRID_W)
    cp, sp = _dft(seq)
    cr, sr = _dft(rows)
    blk = np.kron(np.eye(TM // CHUNK), np.tril(np.ones((CHUNK, CHUNK))))
    bf = lambda a: jnp.asarray(a, F32).astype(BF16)
    return {
        "cch": bf(np.concatenate([cc, sc], axis=1)),
        "mc": bf(np.block([[cg, -sg], [-sg, -cg]])),
        "ltri": jnp.asarray(blk, F32), "utri": jnp.asarray(blk.T, F32),
        "cp": bf(cp), "sp": bf(sp), "cr": bf(cr), "sr": bf(sr),
    }


def _layer_weights(l, w_in, ln_g, conv_w, a_log, dt_bias, o_norm_g, sgu_norm_g, w_spatial, b_spatial, w_branch, w_out):
    d = w_in.shape[1]
    wl = w_in[l]
    nq = 3 * W_BR
    pad = jnp.zeros((2, 128 - 4 * H_A), F32)
    adt = jnp.concatenate([jnp.zeros((2, 2 * H_A), F32),
                           jnp.stack([a_log[l].reshape(-1), dt_bias[l].reshape(-1)]), pad], axis=1)
    return {
        "ln_g": ln_g[l].reshape(1, d),
        "w_qkv": wl[:, :nq].astype(BF16),
        "w_bd": jnp.concatenate([wl[:, nq:nq + 4 * H_A], jnp.zeros((d, 128 - 4 * H_A), F32)], axis=1).astype(BF16),
        "w_rest": wl[:, nq + 4 * H_A:].astype(BF16),
        "conv_w": conv_w[l],
        "adt": adt,
        "sgu_g": sgu_norm_g[l].reshape(1, W_BR),
        "w_sp": w_spatial[l].astype(BF16),
        "b_sp": jnp.broadcast_to(b_spatial[l][:, :, None], (G_B, CHUNK_B, W_BR // G_B)),
        "o_norm_g": o_norm_g[l].reshape(1, DK),
        "w_branch": w_branch[l].astype(BF16),
        "w_out": w_out[l].astype(BF16),
    }


def _group_forward(x, mods, row_of_tile, s0_all, grid_cols, weights, consts, final_g):
    bsz, t, d = x.shape
    ntok = bsz * t
    nb = t // TM
    h = x.reshape(ntok, d)
    states = []
    depth = len(weights)
    for l in range(depth):
        lw = weights[l]
        shift, scale, gate = mods[l, 0], mods[l, 1], mods[l, 2]
        q, k, v, bd, sza, osgu, re, im, szc, sg = _in_call(h, shift, scale, row_of_tile, nb, grid_cols, lw, consts)
        bdt = bd[:, 2 * H_A:4 * H_A].reshape(ntok // CHUNK, CHUNK, H_A, 2).transpose(0, 2, 3, 1).reshape(
            ntok // CHUNK, H_A, PAIR)
        o_f, o_b, s_fin = _delta_call(q, k, v, bd, bdt, s0_all[:, l], nb)
        states.append(s_fin)
        if grid_cols:
            rows = t // GRID_W
            shp = (bsz, rows, GRID_W * W_BR)
            oc = _fourier_call(consts["cr"], consts["sr"], re.reshape(shp), im.reshape(shp), szc.reshape(shp), 4096)
        else:
            shp = (bsz, t, W_BR)
            oc = _fourier_call(consts["cp"], consts["sp"], re.reshape(shp), im.reshape(shp), szc.reshape(shp), W_BR)
        h = _merge_call(h, o_f, o_b, sza, osgu, oc.reshape(ntok, W_BR), sg, gate, row_of_tile, lw,
                        final_g.reshape(1, d), l == depth - 1)
    return h.reshape(bsz, t, d), states


def kernel(x_prompt, x_sample, state_delta, c, c_ctx, ln_g, w_ada, b_ada, w_in, conv_w, a_log, dt_bias, o_norm_g,
           sgu_norm_g, w_spatial, b_spatial, w_branch, w_out, final_g):
    depth, d = ln_g.shape
    bp, tp, _ = x_prompt.shape
    bs, ts, _ = x_sample.shape
    consts = _consts(tp, ts // GRID_W)
    weights = [_layer_weights(l, w_in, ln_g, conv_w, a_log, dt_bias, o_norm_g, sgu_norm_g, w_spatial, b_spatial,
                              w_branch, w_out) for l in range(depth)]
    nrow = -(-(1 + bs) // 8) * 8
    cvecs = jnp.concatenate([c_ctx[None], c, jnp.zeros((nrow - 1 - bs, d), F32)], axis=0)
    mods = _ada_call(cvecs, w_ada, b_ada).reshape(depth, nrow, 3, 1, d).transpose(0, 2, 1, 3, 4)

    zeros = jnp.zeros((bp, depth, 2, H_A, DK, DK), F32)
    y_prompt, st = _group_forward(x_prompt, mods, lambda i: 0, zeros, False, weights, consts, final_g)
    tiles_s = ts // TM
    y_sample, _ = _group_forward(x_sample, mods, lambda i: 1 + i // tiles_s, state_delta, True, weights, consts,
                                 final_g)
    return y_prompt, y_sample, jnp.stack(st, axis=1)
```

```python
import functools

import numpy as np
import jax
import jax.numpy as jnp
from jax import lax
from jax.experimental import pallas as pl
from jax.experimental.pallas import tpu as pltpu

F32 = jnp.float32
BF16 = jnp.bfloat16

W_BR = 512
H_A = 4
DK = 128
CONV_K = 3
CHUNK = 64
PAIR = 2 * CHUNK
G_B = 4
CHUNK_B = 128
G_C = 4
CH_C = W_BR // G_C
GRID_W = 64
COLS_SLAB = 16
EPS = 1e-6
TM = 256
HALO = 8
VMEM_LIMIT = 56 * 1024 * 1024


def _bdot(a, b):
    return jnp.dot(a.astype(BF16), b.astype(BF16), preferred_element_type=F32)


def _split(x):
    hi = x.astype(BF16)
    lo = (x - hi.astype(F32)).astype(BF16)
    return hi, lo


def _dot3(x, y):
    xh, xl = _split(x)
    yh, yl = _split(y)
    n = y.shape[1]
    rhs = jnp.concatenate([jnp.concatenate([yh, yl], axis=1),
                           jnp.concatenate([yh, jnp.zeros_like(yh)], axis=1)], axis=0)
    z = jnp.dot(jnp.concatenate([xh, xl], axis=1), rhs, preferred_element_type=F32)
    return z[:, :n] + z[:, n:]


def _sigmoid(x):
    return 0.5 * jnp.tanh(0.5 * x) + 0.5


def _silu(x):
    return x * _sigmoid(x)


def _cparams(sem):
    return pltpu.CompilerParams(dimension_semantics=sem, vmem_limit_bytes=VMEM_LIMIT)


def _ada_kernel(c_ref, w_ref, b_ref, o_ref):
    o_ref[...] = _bdot(_silu(c_ref[...]), w_ref[...]) + b_ref[...]


def _ada_call(cvecs, w_ada, b_ada):
    depth, d, e = w_ada.shape
    r = cvecs.shape[0]
    tn = 1024
    return pl.pallas_call(
        _ada_kernel,
        out_shape=jax.ShapeDtypeStruct((depth, r, e), F32),
        grid=(depth, e // tn),
        in_specs=[pl.BlockSpec((r, d), lambda l, n: (0, 0)),
                  pl.BlockSpec((None, d, tn), lambda l, n: (l, 0, n)),
                  pl.BlockSpec((None, 1, tn), lambda l, n: (l, 0, n))],
        out_specs=pl.BlockSpec((None, r, tn), lambda l, n: (l, 0, n)),
        compiler_params=_cparams(("arbitrary", "arbitrary")),
        name="ada_mod",
    )(cvecs, w_ada, b_ada.reshape(depth, 1, e))


def _in_kernel(x_ref, xp_ref, xn_ref, shift_ref, scale_ref, lng_ref, wqkv_ref, wbd_ref, wrest_ref,
               convw_ref, adt_ref, sgug_ref, wsp_ref, bsp_ref, cch_ref, mc_ref, ltri_ref, utri_ref,
               q_ref, k_ref, v_ref, bd_ref, sza_ref, osgu_ref, re_ref, im_ref, szc_ref, sg_ref,
               *, seq_tiles, grid_cols):
    i = pl.program_id(0)
    t_in_seq = i % seq_tiles
    lng = lng_ref[...]
    scale1 = 1.0 + scale_ref[...]
    shift = shift_ref[...]

    def normmod(x):
        y = x * lax.rsqrt(jnp.mean(x * x, axis=-1, keepdims=True) + EPS)
        return ((y * lng) * scale1 + shift).astype(BF16)

    xn = normmod(x_ref[...])

    wqkv = wqkv_ref[...]
    p = jnp.dot(xn, wqkv, preferred_element_type=F32)
    p_prev = jnp.dot(normmod(xp_ref[...]), wqkv, preferred_element_type=F32)[HALO - 1:HALO]
    p_next = jnp.dot(normmod(xn_ref[...]), wqkv, preferred_element_type=F32)[0:1]
    p_prev = jnp.where(t_in_seq == 0, 0.0, p_prev)
    p_next = jnp.where(t_in_seq == seq_tiles - 1, 0.0, p_next)
    row = lax.broadcasted_iota(jnp.int32, (TM, 1), 0)
    down = jnp.where(row == 0, p_prev, pltpu.roll(p, 1, axis=0))
    up = jnp.where(row == TM - 1, p_next, pltpu.roll(p, TM - 1, axis=0))
    cw = convw_ref[...]
    y = _silu(cw[0:1] * down + cw[1:2] * p + cw[2:3] * up)
    for h in range(H_A):
        qh = y[:, h * DK:(h + 1) * DK]
        kh = y[:, W_BR + h * DK:W_BR + (h + 1) * DK]
        q_ref[h] = qh * lax.rsqrt(jnp.sum(qh * qh, axis=-1, keepdims=True) + EPS) * (DK ** -0.5)
        k_ref[h] = kh * lax.rsqrt(jnp.sum(kh * kh, axis=-1, keepdims=True) + EPS)
        v_ref[h] = y[:, 2 * W_BR + h * DK:2 * W_BR + (h + 1) * DK]

    pbd = jnp.dot(xn, wbd_ref[...], preferred_element_type=F32)
    adt = adt_ref[...]
    sp_in = pbd + adt[1:2]
    softplus = jnp.maximum(sp_in, 0.0) + jnp.log1p(jnp.exp(-jnp.abs(sp_in)))
    g = -jnp.exp(adt[0:1]) * softplus
    lane = lax.broadcasted_iota(jnp.int32, (TM, 128), 1)
    g = jnp.where((lane >= 2 * H_A) & (lane < 4 * H_A), g, 0.0)
    g1 = g.astype(BF16).astype(F32)
    g2 = (g - g1).astype(BF16).astype(F32)
    g3 = (g - g1 - g2).astype(BF16).astype(F32)
    packed = (g1 + pltpu.roll(g2, 2 * H_A, axis=1) + pltpu.roll(g3, 4 * H_A, axis=1)).astype(BF16)

    def unpack(r):
        return r + pltpu.roll(r, 128 - 2 * H_A, axis=1) + pltpu.roll(r, 128 - 4 * H_A, axis=1)

    gpre = unpack(jnp.dot(ltri_ref[...], packed, preferred_element_type=F32))
    gsuf = unpack(jnp.dot(utri_ref[...], packed, preferred_element_type=F32))
    bd_ref[...] = jnp.where(lane < 2 * H_A, _sigmoid(pbd), jnp.where(lane < 3 * H_A, gpre, gsuf))

    def rest(j):
        return jnp.dot(xn, wrest_ref[:, j * W_BR:(j + 1) * W_BR], preferred_element_type=F32)

    sza_ref[...] = _silu(rest(0))

    u_b, v_b, z_b = rest(1), rest(2), rest(3)
    vn = (v_b * lax.rsqrt(jnp.mean(v_b * v_b, axis=-1, keepdims=True) + EPS) * sgug_ref[...]).astype(BF16)
    gate_b = _silu(z_b)
    for n in range(TM // CHUNK_B):
        rs = slice(n * CHUNK_B, (n + 1) * CHUNK_B)
        for gi in range(G_B):
            cs = slice(gi * (W_BR // G_B), (gi + 1) * (W_BR // G_B))
            vs = jnp.dot(wsp_ref[gi], vn[rs, cs], preferred_element_type=F32) + bsp_ref[gi]
            osgu_ref[rs, cs] = ((u_b[rs, cs] * vs) * gate_b[rs, cs]).astype(BF16)

    x_c = rest(4).astype(BF16)
    szc_ref[...] = _silu(rest(5))
    cch = cch_ref[...]
    a_parts, b_parts = [], []
    for gi in range(G_C):
        ab = jnp.dot(x_c[:, gi * CH_C:(gi + 1) * CH_C], cch, preferred_element_type=F32)
        a_parts.append(ab[:, :CH_C])
        b_parts.append(ab[:, CH_C:])
    a = jnp.concatenate(a_parts, axis=1)
    b = jnp.concatenate(b_parts, axis=1)
    if grid_cols:
        mc = mc_ref[...]
        for r in range(TM // GRID_W):
            rs = slice(r * GRID_W, (r + 1) * GRID_W)
            stacked = jnp.concatenate([a[rs], b[rs]], axis=0).astype(BF16)
            z = jnp.dot(mc, stacked, preferred_element_type=F32)
            re_ref[rs, :] = z[:GRID_W].astype(BF16)
            im_ref[rs, :] = z[GRID_W:].astype(BF16)
    else:
        re_ref[...] = a.astype(BF16)
        im_ref[...] = (-b).astype(BF16)

    for n in range(3):
        for j in range(2):
            c0 = 6 * W_BR + (2 * n + j) * W_BR
            sg_ref[n, :, j * W_BR:(j + 1) * W_BR] = _sigmoid(
                jnp.dot(xn, wrest_ref[:, c0:c0 + W_BR], preferred_element_type=F32))


def _in_call(h, shift, scale, row_of_tile, seq_tiles, grid_cols, lw, consts):
    ntok, d = h.shape
    nt = ntok // TM
    nhb = ntok // HALO
    full = lambda a: pl.BlockSpec(a.shape, lambda i: (0,) * a.ndim)
    tile = lambda w: pl.BlockSpec((TM, w), lambda i: (i, 0))
    mod = pl.BlockSpec((None, 1, d), lambda i: (row_of_tile(i), 0, 0))
    ins = [h, h, h, shift, scale, lw["ln_g"], lw["w_qkv"], lw["w_bd"], lw["w_rest"], lw["conv_w"], lw["adt"],
           lw["sgu_g"], lw["w_sp"], lw["b_sp"], consts["cch"], consts["mc"], consts["ltri"], consts["utri"]]
    in_specs = [tile(d),
                pl.BlockSpec((HALO, d), lambda i: (jnp.maximum(i * (TM // HALO) - 1, 0), 0)),
                pl.BlockSpec((HALO, d), lambda i: (jnp.minimum((i + 1) * (TM // HALO), nhb - 1), 0)),
                mod, mod] + [full(a) for a in ins[5:]]
    hsplit = pl.BlockSpec((H_A, TM, DK), lambda i: (0, i, 0))
    out_shape = [jax.ShapeDtypeStruct((H_A, ntok, DK), F32)] * 3 + [
        jax.ShapeDtypeStruct((ntok, 128), F32),
        jax.ShapeDtypeStruct((ntok, W_BR), F32),
        jax.ShapeDtypeStruct((ntok, W_BR), BF16),
        jax.ShapeDtypeStruct((ntok, W_BR), BF16),
        jax.ShapeDtypeStruct((ntok, W_BR), BF16),
        jax.ShapeDtypeStruct((ntok, W_BR), F32),
        jax.ShapeDtypeStruct((3, ntok, 2 * W_BR), F32),
    ]
    out_specs = [hsplit] * 3 + [tile(128), tile(W_BR), tile(W_BR), tile(W_BR), tile(W_BR), tile(W_BR),
                                pl.BlockSpec((3, TM, 2 * W_BR), lambda i: (0, i, 0))]
    return pl.pallas_call(
        functools.partial(_in_kernel, seq_tiles=seq_tiles, grid_cols=grid_cols),
        out_shape=out_shape, grid=(nt,), in_specs=in_specs, out_specs=out_specs,
        compiler_params=_cparams(("arbitrary",)), name="in_proj",
    )(*ins)


def _delta_kernel(qf_ref, kf_ref, vf_ref, bdf_ref, bdtf_ref, qb_ref, kb_ref, vb_ref, bdb_ref, bdtb_ref, s0_ref,
                  of_ref, ob_ref, s_ref):
    j = pl.program_id(1)

    @pl.when(j == 0)
    def _():
        s_ref[...] = s0_ref[...]

    nch = TM // CHUNK
    ri = lax.broadcasted_iota(jnp.int32, (PAIR, PAIR), 0)
    ci = lax.broadcasted_iota(jnp.int32, (PAIR, PAIR), 1)
    same = (ri < CHUNK) == (ci < CHUNK)
    eye = (ri == ci).astype(F32)
    masks = (same & (ri > ci), same & (ri < ci))
    top = lax.broadcasted_iota(jnp.int32, (PAIR, 1), 0) < CHUNK
    refs = ((qf_ref, kf_ref, vf_ref, bdf_ref, bdtf_ref, of_ref), (qb_ref, kb_ref, vb_ref, bdb_ref, bdtb_ref, ob_ref))
    tr_rhs = (((1,), (1,)), ((), ()))
    tr_lhs = (((0,), (0,)), ((), ()))

    def blockdiag(x):
        return jnp.concatenate([jnp.where(top, x, 0.0), jnp.where(top, 0.0, x)], axis=1)

    def step(s, carry):
        probs = [(d, p) for d in range(2) for p in range(H_A // 2)]
        ld = []
        for d, p in probs:
            q_ref, k_ref, v_ref, bd_ref, bdt_ref, _ = refs[d]
            c = s if d == 0 else nch - 1 - s
            sl = pl.ds(pl.multiple_of(c * CHUNK, CHUNK), CHUNK)
            pair = lambda r: jnp.concatenate([r[2 * p, sl, :], r[2 * p + 1, sl, :]], axis=0)
            col = d * H_A + 2 * p
            colpair = lambda c0: jnp.concatenate([bd_ref[sl, c0:c0 + 1], bd_ref[sl, c0 + 1:c0 + 2]], axis=0)
            gam = colpair(2 * H_A + col)
            last = CHUNK - 1 if d == 0 else 0
            tots = [gam[e * CHUNK + last:e * CHUNK + last + 1] for e in range(2)]
            ld.append(dict(q=pair(q_ref), k=pair(k_ref), v=pair(v_ref), beta=colpair(col), gam=gam,
                           gam_row=bdt_ref[c, 2 * d + p:2 * d + p + 1, :], tots=tots,
                           st=s_ref[d, 2 * p:2 * p + 2].reshape(2 * DK, DK), sl=sl))
        n = len(probs)
        qkk, dec = [], []
        for i, (d, p) in enumerate(probs):
            x = ld[i]
            kb = x["k"].astype(BF16)
            qkk.append(lax.dot_general(jnp.concatenate([x["q"].astype(BF16), kb], axis=0), kb, tr_rhs,
                                       preferred_element_type=F32))
            m = masks[d]
            dec.append(jnp.where(m, jnp.exp(jnp.where(m, x["gam"] - x["gam_row"], 0.0)), 0.0))
        pw = [-((ld[i]["beta"] * qkk[i][PAIR:]) * dec[i]) for i in range(n)]
        t = [eye + pw[i] for i in range(n)]
        pw = [_dot3(pw[i], pw[i]) for i in range(n)]
        for _ in range(4):
            tp = [_dot3(jnp.concatenate([t[i], pw[i]], axis=0), pw[i]) for i in range(n)]
            t = [t[i] + tp[i][:PAIR] for i in range(n)]
            pw = [tp[i][PAIR:] for i in range(n)]
        t = [t[i] + _dot3(t[i], pw[i]) for i in range(n)]
        eg = [jnp.exp(ld[i]["gam"]) for i in range(n)]
        sol = [_dot3(t[i], jnp.concatenate([ld[i]["v"] * ld[i]["beta"], ld[i]["k"] * (ld[i]["beta"] * eg[i])], axis=1))
               for i in range(n)]
        wq = []
        for i in range(n):
            lhs = jnp.concatenate([blockdiag(sol[i][:, DK:]), blockdiag(ld[i]["q"] * eg[i])], axis=0)
            wq.append(jnp.dot(lhs.astype(BF16), ld[i]["st"].astype(BF16), preferred_element_type=F32))
        ub = [(sol[i][:, :DK] - wq[i][:PAIR]).astype(BF16) for i in range(n)]
        outs, snew = [], []
        for i in range(n):
            x = ld[i]
            qk = (qkk[i][:PAIR] * (dec[i] + eye)).astype(BF16)
            outs.append(wq[i][PAIR:] + jnp.dot(qk, ub[i], preferred_element_type=F32))
            tot = jnp.concatenate([jnp.broadcast_to(tt, (CHUNK, 1)) for tt in x["tots"]], axis=0)
            kd = blockdiag(x["k"] * jnp.exp(tot - x["gam"])).astype(BF16)
            gl = jnp.concatenate([jnp.broadcast_to(jnp.exp(tt), (DK, 1)) for tt in x["tots"]], axis=0)
            snew.append(gl * x["st"] + lax.dot_general(kd, ub[i], tr_lhs, preferred_element_type=F32))
        for i, (d, p) in enumerate(probs):
            o_ref = refs[d][5]
            for e in range(2):
                hh = 2 * p + e
                o_ref[ld[i]["sl"], hh * DK:(hh + 1) * DK] = outs[i][e * CHUNK:(e + 1) * CHUNK]
                s_ref[d, hh] = snew[i][e * DK:(e + 1) * DK]
        return carry

    lax.fori_loop(0, nch, step, 0)


def _delta_call(q, k, v, bd, bdt, s0, nb):
    ntok = bd.shape[0]
    bsz = ntok // (nb * TM)
    nch = TM // CHUNK
    fw = lambda b, j: b * nb + j
    bw = lambda b, j: b * nb + nb - 1 - j

    def specs(t):
        hs = pl.BlockSpec((H_A, TM, DK), lambda b, j: (0, t(b, j), 0))
        return [hs, hs, hs, pl.BlockSpec((TM, 128), lambda b, j: (t(b, j), 0)),
                pl.BlockSpec((nch, H_A, PAIR), lambda b, j: (t(b, j), 0, 0))]

    sspec = pl.BlockSpec((None, 2, H_A, DK, DK), lambda b, j: (b, 0, 0, 0, 0))
    return pl.pallas_call(
        _delta_kernel,
        out_shape=[jax.ShapeDtypeStruct((ntok, W_BR), F32), jax.ShapeDtypeStruct((ntok, W_BR), F32),
                   jax.ShapeDtypeStruct((bsz, 2, H_A, DK, DK), F32)],
        grid=(bsz, nb),
        in_specs=specs(fw) + specs(bw) + [sspec],
        out_specs=[pl.BlockSpec((TM, W_BR), lambda b, j: (fw(b, j), 0)),
                   pl.BlockSpec((TM, W_BR), lambda b, j: (bw(b, j), 0)), sspec],
        compiler_params=_cparams(("arbitrary", "arbitrary")), name="delta_scan",
    )(q, k, v, bd, bdt, q, k, v, bd, bdt, s0)


def _fourier_kernel(cr_ref, sr_ref, re_ref, im_ref, szc_ref, o_ref):
    y = (jnp.dot(cr_ref[...], re_ref[...], preferred_element_type=F32)
         + jnp.dot(sr_ref[...], im_ref[...], preferred_element_type=F32))
    o_ref[...] = (y * szc_ref[...]).astype(BF16)


def _fourier_call(cr, sr, re, im, szc, tn):
    bsz, r, n = re.shape
    blk = pl.BlockSpec((None, r, tn), lambda b, c: (b, 0, c))
    mat = pl.BlockSpec((r, r), lambda b, c: (0, 0))
    return pl.pallas_call(
        _fourier_kernel, out_shape=jax.ShapeDtypeStruct((bsz, r, n), BF16),
        grid=(bsz, n // tn), in_specs=[mat, mat, blk, blk, blk], out_specs=blk,
        compiler_params=_cparams(("arbitrary", "arbitrary")), name="fourier_rows",
    )(cr, sr, re, im, szc)


def _fourier_grid_kernel(cr_ref, sr_ref, re_ref, im_ref, szc_ref, o_ref):
    r, s, w = re_ref.shape
    y = (jnp.dot(cr_ref[...], re_ref[...].reshape(r * s, w), preferred_element_type=F32)
         + jnp.dot(sr_ref[...], im_ref[...].reshape(r * s, w), preferred_element_type=F32))
    o_ref[...] = (y * szc_ref[...].reshape(r * s, w)).astype(BF16).reshape(r, s, w)


def _fourier_grid_call(crk, srk, re, im, szc):
    bsz, r, gw, w = re.shape
    blk = pl.BlockSpec((None, r, COLS_SLAB, w), lambda b, c: (b, 0, c, 0))
    mat = pl.BlockSpec(crk.shape, lambda b, c: (0, 0))
    return pl.pallas_call(
        _fourier_grid_kernel, out_shape=jax.ShapeDtypeStruct((bsz, r, gw, w), BF16),
        grid=(bsz, gw // COLS_SLAB), in_specs=[mat, mat, blk, blk, blk], out_specs=blk,
        compiler_params=_cparams(("arbitrary", "arbitrary")), name="fourier_grid_rows",
    )(crk, srk, re, im, szc)


def _merge_kernel(h_ref, of_ref, ob_ref, sza_ref, osgu_ref, oc_ref, sg_ref, gate_ref, ong_ref, wbr_ref, wout_ref,
                  fing_ref, *out_refs, final):
    o = of_ref[...] + ob_ref[...]
    sza = sza_ref[...]
    ong = ong_ref[...]
    parts = []
    for hh in range(H_A):
        oh = o[:, hh * DK:(hh + 1) * DK]
        oh = (oh * lax.rsqrt(jnp.mean(oh * oh, axis=-1, keepdims=True) + EPS)) * ong
        parts.append((oh * sza[:, hh * DK:(hh + 1) * DK]).astype(BF16))
    o_a = jnp.concatenate(parts, axis=1)
    merged = (sg_ref[0] * jnp.dot(o_a, wbr_ref[0], preferred_element_type=F32)
              + sg_ref[1] * jnp.dot(osgu_ref[...], wbr_ref[1], preferred_element_type=F32)
              + sg_ref[2] * jnp.dot(oc_ref[...], wbr_ref[2], preferred_element_type=F32))
    out = jnp.dot(merged.astype(BF16), wout_ref[...], preferred_element_type=F32)
    hn = h_ref[...] + gate_ref[...] * out
    if final:
        out_refs[0][...] = (hn * lax.rsqrt(jnp.mean(hn * hn, axis=-1, keepdims=True) + EPS)) * fing_ref[...]
    else:
        out_refs[0][...] = hn


def _merge_call(h, o_f, o_b, sza, osgu, oc, sg, gate, row_of_tile, lw, final_g, final):
    ntok, d = h.shape
    full = lambda a: pl.BlockSpec(a.shape, lambda i: (0,) * a.ndim)
    tile = lambda w: pl.BlockSpec((TM, w), lambda i: (i, 0))
    ins = [h, o_f, o_b, sza, osgu, oc, sg, gate, lw["o_norm_g"], lw["w_branch"], lw["w_out"], final_g]
    in_specs = [tile(d), tile(W_BR), tile(W_BR), tile(W_BR), tile(W_BR), tile(W_BR),
                pl.BlockSpec((3, TM, d), lambda i: (0, i, 0)),
                pl.BlockSpec((None, 1, d), lambda i: (row_of_tile(i), 0, 0))] + [full(a) for a in ins[8:]]
    return pl.pallas_call(
        functools.partial(_merge_kernel, final=final),
        out_shape=jax.ShapeDtypeStruct((ntok, d), F32), grid=(ntok // TM,),
        in_specs=in_specs, out_specs=tile(d),
        compiler_params=_cparams(("arbitrary",)), name="merge_out",
    )(*ins)


def _dft(n):
    idx = np.arange(n)
    ang = 2.0 * np.pi * ((idx[:, None] * idx[None, :]) % n) / n
    return np.cos(ang) / np.sqrt(n), np.sin(ang) / np.sqrt(n)


def _consts(seq, rows):
    cc, sc = _dft(CH_C)
    cg, sg = _dft(GRID_W)
    cp, sp = _dft(seq)
    cr, sr = _dft(rows)
    blk = np.kron(np.eye(TM // CHUNK), np.tril(np.ones((CHUNK, CHUNK))))
    bf = lambda a: jnp.asarray(a, F32).astype(BF16)
    return {
        "cch": bf(np.concatenate([cc, sc], axis=1)),
        "mc": bf(np.block([[cg, -sg], [-sg, -cg]])),
        "ltri": bf(blk), "utri": bf(blk.T),
        "cp": bf(cp), "sp": bf(sp),
        "crk": bf(np.kron(cr, np.eye(COLS_SLAB))), "srk": bf(np.kron(sr, np.eye(COLS_SLAB))),
    }


def _layer_weights(l, w_in, ln_g, conv_w, a_log, dt_bias, o_norm_g, sgu_norm_g, w_spatial, b_spatial, w_branch, w_out):
    d = w_in.shape[1]
    wl = w_in[l]
    nq = 3 * W_BR
    pad = jnp.zeros((2, 128 - 4 * H_A), F32)
    adt = jnp.concatenate([jnp.zeros((2, 2 * H_A), F32),
                           jnp.stack([a_log[l].reshape(-1), dt_bias[l].reshape(-1)]), pad], axis=1)
    return {
        "ln_g": ln_g[l].reshape(1, d),
        "w_qkv": wl[:, :nq].astype(BF16),
        "w_bd": jnp.concatenate([wl[:, nq:nq + 4 * H_A], jnp.zeros((d, 128 - 4 * H_A), F32)], axis=1).astype(BF16),
        "w_rest": wl[:, nq + 4 * H_A:].astype(BF16),
        "conv_w": conv_w[l],
        "adt": adt,
        "sgu_g": sgu_norm_g[l].reshape(1, W_BR),
        "w_sp": w_spatial[l].astype(BF16),
        "b_sp": jnp.broadcast_to(b_spatial[l][:, :, None], (G_B, CHUNK_B, W_BR // G_B)),
        "o_norm_g": o_norm_g[l].reshape(1, DK),
        "w_branch": w_branch[l].astype(BF16),
        "w_out": w_out[l].astype(BF16),
    }


def _group_forward(x, mods, row_of_tile, s0_all, grid_cols, weights, consts, final_g):
    bsz, t, d = x.shape
    ntok = bsz * t
    nb = t // TM
    h = x.reshape(ntok, d)
    states = []
    depth = len(weights)
    for l in range(depth):
        lw = weights[l]
        shift, scale, gate = mods[l, 0], mods[l, 1], mods[l, 2]
        q, k, v, bd, sza, osgu, re, im, szc, sg = _in_call(h, shift, scale, row_of_tile, nb, grid_cols, lw, consts)
        bdt = bd[:, 2 * H_A:4 * H_A].reshape(ntok // CHUNK, CHUNK, H_A, 2).transpose(0, 2, 3, 1).reshape(
            ntok // CHUNK, H_A, PAIR)
        o_f, o_b, s_fin = _delta_call(q, k, v, bd, bdt, s0_all[:, l], nb)
        states.append(s_fin)
        if grid_cols:
            shp = (bsz, t // GRID_W, GRID_W, W_BR)
            oc = _fourier_grid_call(consts["crk"], consts["srk"], re.reshape(shp), im.reshape(shp), szc.reshape(shp))
        else:
            shp = (bsz, t, W_BR)
            oc = _fourier_call(consts["cp"], consts["sp"], re.reshape(shp), im.reshape(shp), szc.reshape(shp), W_BR)
        h = _merge_call(h, o_f, o_b, sza, osgu, oc.reshape(ntok, W_BR), sg, gate, row_of_tile, lw,
                        final_g.reshape(1, d), l == depth - 1)
    return h.reshape(bsz, t, d), states


def kernel(x_prompt, x_sample, state_delta, c, c_ctx, ln_g, w_ada, b_ada, w_in, conv_w, a_log, dt_bias, o_norm_g,
           sgu_norm_g, w_spatial, b_spatial, w_branch, w_out, final_g):
    depth, d = ln_g.shape
    bp, tp, _ = x_prompt.shape
    bs, ts, _ = x_sample.shape
    consts = _consts(tp, ts // GRID_W)
    weights = [_layer_weights(l, w_in, ln_g, conv_w, a_log, dt_bias, o_norm_g, sgu_norm_g, w_spatial, b_spatial,
                              w_branch, w_out) for l in range(depth)]
    nrow = -(-(1 + bs) // 8) * 8
    cvecs = jnp.concatenate([c_ctx[None], c, jnp.zeros((nrow - 1 - bs, d), F32)], axis=0)
    mods = _ada_call(cvecs, w_ada, b_ada).reshape(depth, nrow, 3, 1, d).transpose(0, 2, 1, 3, 4)

    zeros = jnp.zeros((bp, depth, 2, H_A, DK, DK), F32)
    y_prompt, st = _group_forward(x_prompt, mods, lambda i: 0, zeros, False, weights, consts, final_g)
    tiles_s = ts // TM
    y_sample, _ = _group_forward(x_sample, mods, lambda i: 1 + i // tiles_s, state_delta, True, weights, consts,
                                 final_g)
    return y_prompt, y_sample, jnp.stack(st, axis=1)
```

```python
import functools

import numpy as np
import jax
import jax.numpy as jnp
from jax import lax
from jax.experimental import pallas as pl
from jax.experimental.pallas import tpu as pltpu

F32 = jnp.float32
BF16 = jnp.bfloat16

W_BR = 512
H_A = 4
DK = 128
CONV_K = 3
CHUNK = 64
PAIR = 2 * CHUNK
G_B = 4
CHUNK_B = 128
G_C = 4
CH_C = W_BR // G_C
GRID_W = 64
COLS_SLAB = 16
EPS = 1e-6
TM = 256
HALO = 16
VMEM_LIMIT = 56 * 1024 * 1024


def _bdot(a, b):
    return jnp.dot(a.astype(BF16), b.astype(BF16), preferred_element_type=F32)


def _split(x):
    hi = x.astype(BF16)
    lo = (x - hi.astype(F32)).astype(BF16)
    return hi, lo


def _sigmoid(x):
    return 0.5 * jnp.tanh(0.5 * x) + 0.5


def _silu(x):
    return x * _sigmoid(x)


def _cparams(sem):
    return pltpu.CompilerParams(dimension_semantics=sem, vmem_limit_bytes=VMEM_LIMIT)


def _layer_spec(a, l):
    return pl.BlockSpec((None,) + a.shape[1:], lambda *_: (l,) + (0,) * (a.ndim - 1))


def _full_spec(a):
    return pl.BlockSpec(a.shape, lambda *_: (0,) * a.ndim)


def _ada_kernel(c_ref, w_ref, b_ref, o_ref):
    o_ref[...] = _bdot(_silu(c_ref[...]), w_ref[...]) + b_ref[...]


def _ada_call(cvecs, w_ada, b_ada):
    depth, d, e = w_ada.shape
    r = cvecs.shape[0]
    tn = 1024
    return pl.pallas_call(
        _ada_kernel,
        out_shape=jax.ShapeDtypeStruct((depth, r, e), F32),
        grid=(depth, e // tn),
        in_specs=[pl.BlockSpec((r, d), lambda l, n: (0, 0)),
                  pl.BlockSpec((None, d, tn), lambda l, n: (l, 0, n)),
                  pl.BlockSpec((None, 1, tn), lambda l, n: (l, 0, n))],
        out_specs=pl.BlockSpec((None, r, tn), lambda l, n: (l, 0, n)),
        compiler_params=_cparams(("arbitrary", "arbitrary")),
        name="ada_mod",
    )(cvecs, w_ada, b_ada.reshape(depth, 1, e))


def _in_kernel(x_ref, xp_ref, xn_ref, shift_ref, scale_ref, lng_ref, wqkv_ref, wbd_ref, wrest_ref,
               convw_ref, adt_ref, sgug_ref, wsp_ref, bsp_ref, cch_ref, mc_ref, ltri_ref, utri_ref,
               q_ref, k_ref, v_ref, bd_ref, sza_ref, osgu_ref, re_ref, im_ref, szc_ref, sg_ref,
               *, seq_tiles, grid_cols):
    i = pl.program_id(0)
    t_in_seq = i % seq_tiles
    lng = lng_ref[...]
    scale1 = 1.0 + scale_ref[...]
    shift = shift_ref[...]

    x = jnp.concatenate([xp_ref[...], x_ref[...], xn_ref[...]], axis=0)
    y = x * lax.rsqrt(jnp.mean(x * x, axis=-1, keepdims=True) + EPS)
    xe = ((y * lng) * scale1 + shift).astype(BF16)
    xn = xe[HALO:HALO + TM]

    pe = jnp.dot(xe, wqkv_ref[...], preferred_element_type=F32)
    p = pe[HALO:HALO + TM]
    p_prev = jnp.where(t_in_seq == 0, 0.0, pe[HALO - 1:HALO])
    p_next = jnp.where(t_in_seq == seq_tiles - 1, 0.0, pe[HALO + TM:HALO + TM + 1])
    row = lax.broadcasted_iota(jnp.int32, (TM, 1), 0)
    down = jnp.where(row == 0, p_prev, pltpu.roll(p, 1, axis=0))
    up = jnp.where(row == TM - 1, p_next, pltpu.roll(p, TM - 1, axis=0))
    cw = convw_ref[...]
    y = _silu(cw[0:1] * down + cw[1:2] * p + cw[2:3] * up)
    for h in range(H_A):
        qh = y[:, h * DK:(h + 1) * DK]
        kh = y[:, W_BR + h * DK:W_BR + (h + 1) * DK]
        q_ref[h] = qh * lax.rsqrt(jnp.sum(qh * qh, axis=-1, keepdims=True) + EPS) * (DK ** -0.5)
        k_ref[h] = kh * lax.rsqrt(jnp.sum(kh * kh, axis=-1, keepdims=True) + EPS)
        v_ref[h] = y[:, 2 * W_BR + h * DK:2 * W_BR + (h + 1) * DK]

    pbd = jnp.dot(xn, wbd_ref[...], preferred_element_type=F32)
    adt = adt_ref[...]
    sp_in = pbd + adt[1:2]
    softplus = jnp.maximum(sp_in, 0.0) + jnp.log1p(jnp.exp(-jnp.abs(sp_in)))
    g = -jnp.exp(adt[0:1]) * softplus
    lane = lax.broadcasted_iota(jnp.int32, (TM, 128), 1)
    g = jnp.where((lane >= 2 * H_A) & (lane < 4 * H_A), g, 0.0)
    g1 = g.astype(BF16).astype(F32)
    g2 = (g - g1).astype(BF16).astype(F32)
    g3 = (g - g1 - g2).astype(BF16).astype(F32)
    packed = (g1 + pltpu.roll(g2, 2 * H_A, axis=1) + pltpu.roll(g3, 4 * H_A, axis=1)).astype(BF16)

    def unpack(r):
        return r + pltpu.roll(r, 128 - 2 * H_A, axis=1) + pltpu.roll(r, 128 - 4 * H_A, axis=1)

    gpre = unpack(jnp.dot(ltri_ref[...], packed, preferred_element_type=F32))
    gsuf = unpack(jnp.dot(utri_ref[...], packed, preferred_element_type=F32))
    bd_ref[...] = jnp.where(lane < 2 * H_A, _sigmoid(pbd), jnp.where(lane < 3 * H_A, gpre, gsuf))

    def rest(j):
        return jnp.dot(xn, wrest_ref[:, j * W_BR:(j + 1) * W_BR], preferred_element_type=F32)

    sza_ref[...] = _silu(rest(0)).astype(BF16)

    u_b, v_b, z_b = rest(1), rest(2), rest(3)
    vn = (v_b * lax.rsqrt(jnp.mean(v_b * v_b, axis=-1, keepdims=True) + EPS) * sgug_ref[...]).astype(BF16)
    gate_b = _silu(z_b)
    for n in range(TM // CHUNK_B):
        rs = slice(n * CHUNK_B, (n + 1) * CHUNK_B)
        for gi in range(G_B):
            cs = slice(gi * (W_BR // G_B), (gi + 1) * (W_BR // G_B))
            vs = jnp.dot(wsp_ref[gi], vn[rs, cs], preferred_element_type=F32) + bsp_ref[gi]
            osgu_ref[rs, cs] = ((u_b[rs, cs] * vs) * gate_b[rs, cs]).astype(BF16)

    x_c = rest(4).astype(BF16)
    szc_ref[...] = _silu(rest(5)).astype(BF16)
    cch = cch_ref[...]
    a_parts, b_parts = [], []
    for gi in range(G_C):
        ab = jnp.dot(x_c[:, gi * CH_C:(gi + 1) * CH_C], cch, preferred_element_type=F32)
        a_parts.append(ab[:, :CH_C])
        b_parts.append(ab[:, CH_C:])
    a = jnp.concatenate(a_parts, axis=1)
    b = jnp.concatenate(b_parts, axis=1)
    if grid_cols:
        mc = mc_ref[...]
        for r in range(TM // GRID_W):
            rs = slice(r * GRID_W, (r + 1) * GRID_W)
            stacked = jnp.concatenate([a[rs], b[rs]], axis=0).astype(BF16)
            z = jnp.dot(mc, stacked, preferred_element_type=F32)
            re_ref[rs, :] = z[:GRID_W].astype(BF16)
            im_ref[rs, :] = z[GRID_W:].astype(BF16)
    else:
        re_ref[...] = a.astype(BF16)
        im_ref[...] = (-b).astype(BF16)

    for n in range(3):
        for j in range(2):
            c0 = 6 * W_BR + (2 * n + j) * W_BR
            sg_ref[n, :, j * W_BR:(j + 1) * W_BR] = _sigmoid(
                jnp.dot(xn, wrest_ref[:, c0:c0 + W_BR], preferred_element_type=F32)).astype(BF16)


def _in_call(h, shift, scale, row_of_tile, seq_tiles, grid_cols, l, wts, consts):
    ntok, d = h.shape
    nt = ntok // TM
    nhb = ntok // HALO
    tile = lambda w: pl.BlockSpec((TM, w), lambda i: (i, 0))
    mod = pl.BlockSpec((None, 1, d), lambda i: (row_of_tile(i), 0, 0))
    lws = [wts[n] for n in ("ln_g", "w_qkv", "w_bd", "w_rest", "conv_w", "adt", "sgu_g", "w_sp", "b_sp")]
    cs = [consts[n] for n in ("cch", "mc", "ltri", "utri")]
    in_specs = [tile(d),
                pl.BlockSpec((HALO, d), lambda i: (jnp.maximum(i * (TM // HALO) - 1, 0), 0)),
                pl.BlockSpec((HALO, d), lambda i: (jnp.minimum((i + 1) * (TM // HALO), nhb - 1), 0)),
                mod, mod] + [_layer_spec(a, l) for a in lws] + [_full_spec(a) for a in cs]
    hsplit = pl.BlockSpec((H_A, TM, DK), lambda i: (0, i, 0))
    out_shape = [jax.ShapeDtypeStruct((H_A, ntok, DK), F32)] * 3 + [
        jax.ShapeDtypeStruct((ntok, 128), F32),
        jax.ShapeDtypeStruct((ntok, W_BR), BF16),
        jax.ShapeDtypeStruct((ntok, W_BR), BF16),
        jax.ShapeDtypeStruct((ntok, W_BR), BF16),
        jax.ShapeDtypeStruct((ntok, W_BR), BF16),
        jax.ShapeDtypeStruct((ntok, W_BR), BF16),
        jax.ShapeDtypeStruct((3, ntok, 2 * W_BR), BF16),
    ]
    out_specs = [hsplit] * 3 + [tile(128), tile(W_BR), tile(W_BR), tile(W_BR), tile(W_BR), tile(W_BR),
                                pl.BlockSpec((3, TM, 2 * W_BR), lambda i: (0, i, 0))]
    return pl.pallas_call(
        functools.partial(_in_kernel, seq_tiles=seq_tiles, grid_cols=grid_cols),
        out_shape=out_shape, grid=(nt,), in_specs=in_specs, out_specs=out_specs,
        compiler_params=_cparams(("arbitrary",)), name="in_proj",
    )(h, h, h, shift, scale, *lws, *cs)


def _dot3(x, y):
    xh, xl = _split(x)
    yh, yl = _split(y)
    n = y.shape[1]
    rhs = jnp.concatenate([jnp.concatenate([yh, yl], axis=1),
                           jnp.concatenate([yh, jnp.zeros_like(yh)], axis=1)], axis=0)
    z = jnp.dot(jnp.concatenate([xh, xl], axis=1), rhs, preferred_element_type=F32)
    return z[:, :n] + z[:, n:]


def _delta_kernel(qf_ref, kf_ref, vf_ref, bdf_ref, bdtf_ref, qb_ref, kb_ref, vb_ref, bdb_ref, bdtb_ref, s0_ref,
                  of_ref, ob_ref, s_ref):
    j = pl.program_id(1)

    @pl.when(j == 0)
    def _():
        s_ref[...] = s0_ref[...]

    nch = TM // CHUNK
    ri = lax.broadcasted_iota(jnp.int32, (PAIR, PAIR), 0)
    ci = lax.broadcasted_iota(jnp.int32, (PAIR, PAIR), 1)
    same = (ri < CHUNK) == (ci < CHUNK)
    eye = (ri == ci).astype(F32)
    masks = (same & (ri > ci), same & (ri < ci))
    top = lax.broadcasted_iota(jnp.int32, (PAIR, 1), 0) < CHUNK
    refs = ((qf_ref, kf_ref, vf_ref, bdf_ref, bdtf_ref, of_ref), (qb_ref, kb_ref, vb_ref, bdb_ref, bdtb_ref, ob_ref))
    tr_rhs = (((1,), (1,)), ((), ()))
    tr_lhs = (((0,), (0,)), ((), ()))
    probs = [(d, p) for d in range(2) for p in range(H_A // 2)]
    n = len(probs)

    def blockdiag(x):
        return jnp.concatenate([jnp.where(top, x, 0.0), jnp.where(top, 0.0, x)], axis=1)

    def local(s):
        ld = []
        for d, p in probs:
            q_ref, k_ref, v_ref, bd_ref, bdt_ref, _ = refs[d]
            c = s if d == 0 else nch - 1 - s
            sl = slice(c * CHUNK, (c + 1) * CHUNK)
            pair = lambda r: jnp.concatenate([r[2 * p, sl, :], r[2 * p + 1, sl, :]], axis=0)
            col = d * H_A + 2 * p
            colpair = lambda c0: jnp.concatenate([bd_ref[sl, c0:c0 + 1], bd_ref[sl, c0 + 1:c0 + 2]], axis=0)
            gam = colpair(2 * H_A + col)
            last = CHUNK - 1 if d == 0 else 0
            tots = [gam[e * CHUNK + last:e * CHUNK + last + 1] for e in range(2)]
            ld.append(dict(q=pair(q_ref), k=pair(k_ref), v=pair(v_ref), beta=colpair(col), gam=gam,
                           gam_row=bdt_ref[c, 2 * d + p:2 * d + p + 1, :], tots=tots, sl=sl))
        qkk, dec = [], []
        for i, (d, p) in enumerate(probs):
            x = ld[i]
            kb = x["k"].astype(BF16)
            qkk.append(lax.dot_general(jnp.concatenate([x["q"].astype(BF16), kb], axis=0), kb, tr_rhs,
                                       preferred_element_type=F32))
            m = masks[d]
            dec.append(jnp.where(m, jnp.exp(jnp.where(m, x["gam"] - x["gam_row"], 0.0)), 0.0))
        pw = [-((ld[i]["beta"] * qkk[i][PAIR:]) * dec[i]) for i in range(n)]
        t = [eye + pw[i] for i in range(n)]
        pw = [_dot3(pw[i], pw[i]) for i in range(n)]
        for _ in range(4):
            tp = [_dot3(jnp.concatenate([t[i], pw[i]], axis=0), pw[i]) for i in range(n)]
            t = [t[i] + tp[i][:PAIR] for i in range(n)]
            pw = [tp[i][PAIR:] for i in range(n)]
        t = [t[i] + _dot3(t[i], pw[i]) for i in range(n)]
        eg = [jnp.exp(ld[i]["gam"]) for i in range(n)]
        sol = [_dot3(t[i], jnp.concatenate([ld[i]["v"] * ld[i]["beta"], ld[i]["k"] * (ld[i]["beta"] * eg[i])], axis=1))
               for i in range(n)]
        out = []
        for i in range(n):
            x = ld[i]
            tot = jnp.concatenate([jnp.broadcast_to(tt, (CHUNK, 1)) for tt in x["tots"]], axis=0)
            out.append(dict(
                lhs=jnp.concatenate([blockdiag(sol[i][:, DK:]), blockdiag(x["q"] * eg[i])], axis=0).astype(BF16),
                u0=sol[i][:, :DK],
                qk=(qkk[i][:PAIR] * (dec[i] + eye)).astype(BF16),
                kd=blockdiag(x["k"] * jnp.exp(tot - x["gam"])).astype(BF16),
                gl=jnp.concatenate([jnp.broadcast_to(jnp.exp(tt), (DK, 1)) for tt in x["tots"]], axis=0),
                sl=x["sl"]))
        return out

    def scan(loc, st):
        wq = [jnp.dot(loc[i]["lhs"], st[i].astype(BF16), preferred_element_type=F32) for i in range(n)]
        ub = [(loc[i]["u0"] - wq[i][:PAIR]).astype(BF16) for i in range(n)]
        new = []
        for i, (d, p) in enumerate(probs):
            o = wq[i][PAIR:] + jnp.dot(loc[i]["qk"], ub[i], preferred_element_type=F32)
            new.append(loc[i]["gl"] * st[i] + lax.dot_general(loc[i]["kd"], ub[i], tr_lhs,
                                                              preferred_element_type=F32))
            for e in range(2):
                hh = 2 * p + e
                refs[d][5][loc[i]["sl"], hh * DK:(hh + 1) * DK] = o[e * CHUNK:(e + 1) * CHUNK]
        return new

    st = [s_ref[d, 2 * p:2 * p + 2].reshape(2 * DK, DK) for d, p in probs]
    loc = local(0)
    for s in range(nch):
        nxt = local(s + 1) if s + 1 < nch else None
        st = scan(loc, st)
        loc = nxt
    for i, (d, p) in enumerate(probs):
        s_ref[d, 2 * p:2 * p + 2] = st[i].reshape(2, DK, DK)


def _delta_call(q, k, v, bd, bdt, s0, l, nb):
    ntok = bd.shape[0]
    bsz = ntok // (nb * TM)
    nch = TM // CHUNK
    fw = lambda b, j: b * nb + j
    bw = lambda b, j: b * nb + nb - 1 - j

    def specs(t):
        hs = pl.BlockSpec((H_A, TM, DK), lambda b, j: (0, t(b, j), 0))
        return [hs, hs, hs, pl.BlockSpec((TM, 128), lambda b, j: (t(b, j), 0)),
                pl.BlockSpec((nch, H_A, PAIR), lambda b, j: (t(b, j), 0, 0))]

    return pl.pallas_call(
        _delta_kernel,
        out_shape=[jax.ShapeDtypeStruct((ntok, W_BR), F32), jax.ShapeDtypeStruct((ntok, W_BR), F32),
                   jax.ShapeDtypeStruct((bsz, 2, H_A, DK, DK), F32)],
        grid=(bsz, nb),
        in_specs=specs(fw) + specs(bw) + [pl.BlockSpec((None, None, 2, H_A, DK, DK),
                                                       lambda b, j: (b, l, 0, 0, 0, 0))],
        out_specs=[pl.BlockSpec((TM, W_BR), lambda b, j: (fw(b, j), 0)),
                   pl.BlockSpec((TM, W_BR), lambda b, j: (bw(b, j), 0)),
                   pl.BlockSpec((None, 2, H_A, DK, DK), lambda b, j: (b, 0, 0, 0, 0))],
        compiler_params=_cparams(("arbitrary", "arbitrary")), name="delta_scan",
    )(q, k, v, bd, bdt, q, k, v, bd, bdt, s0)


def _fourier_kernel(cr_ref, sr_ref, re_ref, im_ref, szc_ref, o_ref):
    y = (jnp.dot(cr_ref[...], re_ref[...], preferred_element_type=F32)
         + jnp.dot(sr_ref[...], im_ref[...], preferred_element_type=F32))
    o_ref[...] = (y * szc_ref[...]).astype(BF16)


def _fourier_call(cr, sr, re, im, szc, tn):
    bsz, r, n = re.shape
    blk = pl.BlockSpec((None, r, tn), lambda b, c: (b, 0, c))
    mat = pl.BlockSpec((r, r), lambda b, c: (0, 0))
    return pl.pallas_call(
        _fourier_kernel, out_shape=jax.ShapeDtypeStruct((bsz, r, n), BF16),
        grid=(bsz, n // tn), in_specs=[mat, mat, blk, blk, blk], out_specs=blk,
        compiler_params=_cparams(("arbitrary", "arbitrary")), name="fourier_rows",
    )(cr, sr, re, im, szc)


def _fourier_grid_kernel(cr_ref, sr_ref, re_ref, im_ref, szc_ref, o_ref):
    r, s, w = re_ref.shape
    y = (jnp.dot(cr_ref[...], re_ref[...].reshape(r * s, w), preferred_element_type=F32)
         + jnp.dot(sr_ref[...], im_ref[...].reshape(r * s, w), preferred_element_type=F32))
    o_ref[...] = (y * szc_ref[...].reshape(r * s, w)).astype(BF16).reshape(r, s, w)


def _fourier_grid_call(crk, srk, re, im, szc):
    bsz, r, gw, w = re.shape
    blk = pl.BlockSpec((None, r, COLS_SLAB, w), lambda b, c: (b, 0, c, 0))
    mat = pl.BlockSpec(crk.shape, lambda b, c: (0, 0))
    return pl.pallas_call(
        _fourier_grid_kernel, out_shape=jax.ShapeDtypeStruct((bsz, r, gw, w), BF16),
        grid=(bsz, gw // COLS_SLAB), in_specs=[mat, mat, blk, blk, blk], out_specs=blk,
        compiler_params=_cparams(("arbitrary", "arbitrary")), name="fourier_grid_rows",
    )(crk, srk, re, im, szc)


def _merge_kernel(h_ref, of_ref, ob_ref, sza_ref, osgu_ref, oc_ref, sg_ref, gate_ref, ong_ref, wbr_ref, wout_ref,
                  fing_ref, o_ref, *, final):
    o = of_ref[...] + ob_ref[...]
    sza = sza_ref[...]
    ong = ong_ref[...]
    parts = []
    for hh in range(H_A):
        oh = o[:, hh * DK:(hh + 1) * DK]
        oh = (oh * lax.rsqrt(jnp.mean(oh * oh, axis=-1, keepdims=True) + EPS)) * ong
        parts.append((oh * sza[:, hh * DK:(hh + 1) * DK]).astype(BF16))
    o_a = jnp.concatenate(parts, axis=1)
    merged = (sg_ref[0] * jnp.dot(o_a, wbr_ref[0], preferred_element_type=F32)
              + sg_ref[1] * jnp.dot(osgu_ref[...], wbr_ref[1], preferred_element_type=F32)
              + sg_ref[2] * jnp.dot(oc_ref[...], wbr_ref[2], preferred_element_type=F32))
    out = jnp.dot(merged.astype(BF16), wout_ref[...], preferred_element_type=F32)
    hn = h_ref[...] + gate_ref[...] * out
    if final:
        hn = (hn * lax.rsqrt(jnp.mean(hn * hn, axis=-1, keepdims=True) + EPS)) * fing_ref[...]
    o_ref[...] = hn


def _merge_call(h, o_f, o_b, sza, osgu, oc, sg, gate, row_of_tile, l, wts, final_g, final):
    ntok, d = h.shape
    tile = lambda w: pl.BlockSpec((TM, w), lambda i: (i, 0))
    lws = [wts[n] for n in ("o_norm_g", "w_branch", "w_out")]
    in_specs = [tile(d), tile(W_BR), tile(W_BR), tile(W_BR), tile(W_BR), tile(W_BR),
                pl.BlockSpec((3, TM, d), lambda i: (0, i, 0)),
                pl.BlockSpec((None, 1, d), lambda i: (row_of_tile(i), 0, 0))
                ] + [_layer_spec(a, l) for a in lws] + [_full_spec(final_g)]
    return pl.pallas_call(
        functools.partial(_merge_kernel, final=final),
        out_shape=jax.ShapeDtypeStruct((ntok, d), F32), grid=(ntok // TM,),
        in_specs=in_specs, out_specs=tile(d),
        compiler_params=_cparams(("arbitrary",)), name="merge_out",
    )(h, o_f, o_b, sza, osgu, oc, sg, gate, *lws, final_g)


def _dft(n):
    idx = np.arange(n)
    ang = 2.0 * np.pi * ((idx[:, None] * idx[None, :]) % n) / n
    return np.cos(ang) / np.sqrt(n), np.sin(ang) / np.sqrt(n)


def _consts(seq, rows):
    cc, sc = _dft(CH_C)
    cg, sg = _dft(GRID_W)
    cp, sp = _dft(seq)
    cr, sr = _dft(rows)
    blk = np.kron(np.eye(TM // CHUNK), np.tril(np.ones((CHUNK, CHUNK))))
    bf = lambda a: jnp.asarray(a, F32).astype(BF16)
    return {
        "cch": bf(np.concatenate([cc, sc], axis=1)),
        "mc": bf(np.block([[cg, -sg], [-sg, -cg]])),
        "ltri": bf(blk), "utri": bf(blk.T),
        "cp": bf(cp), "sp": bf(sp),
        "crk": bf(np.kron(cr, np.eye(COLS_SLAB))), "srk": bf(np.kron(sr, np.eye(COLS_SLAB))),
    }


def _stacked_weights(w_in, ln_g, conv_w, a_log, dt_bias, o_norm_g, sgu_norm_g, w_spatial, b_spatial, w_branch, w_out):
    depth, d, _ = w_in.shape
    nq = 3 * W_BR
    nbd = 4 * H_A
    adt = jnp.stack([a_log.reshape(depth, -1), dt_bias.reshape(depth, -1)], axis=1)
    return {
        "ln_g": ln_g.reshape(depth, 1, d),
        "w_qkv": w_in[:, :, :nq].astype(BF16),
        "w_bd": jnp.pad(w_in[:, :, nq:nq + nbd], ((0, 0), (0, 0), (0, 128 - nbd))).astype(BF16),
        "w_rest": w_in[:, :, nq + nbd:].astype(BF16),
        "conv_w": conv_w,
        "adt": jnp.pad(adt, ((0, 0), (0, 0), (2 * H_A, 128 - nbd))),
        "sgu_g": sgu_norm_g.reshape(depth, 1, W_BR),
        "w_sp": w_spatial.astype(BF16),
        "b_sp": jnp.broadcast_to(b_spatial[..., None], b_spatial.shape + (W_BR // G_B,)),
        "o_norm_g": o_norm_g.reshape(depth, 1, DK),
        "w_branch": w_branch.astype(BF16),
        "w_out": w_out.astype(BF16),
    }


def _group_forward(x, mods, row_of_tile, s0_all, grid_cols, wts, consts, final_g):
    bsz, t, d = x.shape
    ntok = bsz * t
    nb = t // TM
    h = x.reshape(ntok, d)
    states = []
    depth = mods.shape[0]
    for l in range(depth):
        shift, scale, gate = mods[l, 0], mods[l, 1], mods[l, 2]
        q, k, v, bd, sza, osgu, re, im, szc, sg = _in_call(h, shift, scale, row_of_tile, nb, grid_cols, l, wts, consts)
        bdt = bd[:, 2 * H_A:4 * H_A].reshape(ntok // CHUNK, CHUNK, H_A, 2).transpose(0, 2, 3, 1).reshape(
            ntok // CHUNK, H_A, PAIR)
        o_f, o_b, s_fin = _delta_call(q, k, v, bd, bdt, s0_all, min(l, s0_all.shape[1] - 1), nb)
        states.append(s_fin)
        if grid_cols:
            shp = (bsz, t // GRID_W, GRID_W, W_BR)
            oc = _fourier_grid_call(consts["crk"], consts["srk"], re.reshape(shp), im.reshape(shp), szc.reshape(shp))
        else:
            shp = (bsz, t, W_BR)
            oc = _fourier_call(consts["cp"], consts["sp"], re.reshape(shp), im.reshape(shp), szc.reshape(shp), W_BR)
        h = _merge_call(h, o_f, o_b, sza, osgu, oc.reshape(ntok, W_BR), sg, gate, row_of_tile, l, wts,
                        final_g.reshape(1, d), l == depth - 1)
    return h.reshape(bsz, t, d), states


def kernel(x_prompt, x_sample, state_delta, c, c_ctx, ln_g, w_ada, b_ada, w_in, conv_w, a_log, dt_bias, o_norm_g,
           sgu_norm_g, w_spatial, b_spatial, w_branch, w_out, final_g):
    depth, d = ln_g.shape
    bp, tp, _ = x_prompt.shape
    bs, ts, _ = x_sample.shape
    consts = _consts(tp, ts // GRID_W)
    wts = _stacked_weights(w_in, ln_g, conv_w, a_log, dt_bias, o_norm_g, sgu_norm_g, w_spatial, b_spatial,
                           w_branch, w_out)
    nrow = -(-(1 + bs) // 8) * 8
    cvecs = jnp.concatenate([c_ctx[None], c, jnp.zeros((nrow - 1 - bs, d), F32)], axis=0)
    mods = _ada_call(cvecs, w_ada, b_ada).reshape(depth, nrow, 3, 1, d).transpose(0, 2, 1, 3, 4)

    zeros = jnp.zeros((bp, 1, 2, H_A, DK, DK), F32)
    y_prompt, st = _group_forward(x_prompt, mods, lambda i: 0, zeros, False, wts, consts, final_g)
    tiles_s = ts // TM
    y_sample, _ = _group_forward(x_sample, mods, lambda i: 1 + i // tiles_s, state_delta, True, wts, consts, final_g)
    return y_prompt, y_sample, jnp.stack(st, axis=1)
```

```python
import functools

import numpy as np
import jax
import jax.numpy as jnp
from jax import lax
from jax.experimental import pallas as pl
from jax.experimental.pallas import tpu as pltpu

F32 = jnp.float32
BF16 = jnp.bfloat16

W_BR = 512
H_A = 4
DK = 128
CONV_K = 3
CHUNK = 64
PAIR = 2 * CHUNK
G_B = 4
CHUNK_B = 128
G_C = 4
CH_C = W_BR // G_C
GRID_W = 64
COLS_SLAB = 16
EPS = 1e-6
TM = 256
TMM = 512
TD = 512
LOCKSTEP = 2
HALO = 16
VMEM_LIMIT = 56 * 1024 * 1024


def _bdot(a, b):
    return jnp.dot(a.astype(BF16), b.astype(BF16), preferred_element_type=F32)


def _split(x):
    hi = x.astype(BF16)
    lo = (x - hi.astype(F32)).astype(BF16)
    return hi, lo


def _sigmoid(x):
    return 0.5 * jnp.tanh(0.5 * x) + 0.5


def _silu(x):
    return x * _sigmoid(x)


def _cparams(sem):
    return pltpu.CompilerParams(dimension_semantics=sem, vmem_limit_bytes=VMEM_LIMIT)


def _layer_spec(a, l):
    return pl.BlockSpec((None,) + a.shape[1:], lambda *_: (l,) + (0,) * (a.ndim - 1))


def _full_spec(a):
    return pl.BlockSpec(a.shape, lambda *_: (0,) * a.ndim)


def _ada_kernel(c_ref, w_ref, b_ref, o_ref):
    o_ref[...] = _bdot(_silu(c_ref[...]), w_ref[...]) + b_ref[...]


def _ada_call(cvecs, w_ada, b_ada):
    depth, d, e = w_ada.shape
    r = cvecs.shape[0]
    tn = 1024
    return pl.pallas_call(
        _ada_kernel,
        out_shape=jax.ShapeDtypeStruct((depth, r, e), F32),
        grid=(depth, e // tn),
        in_specs=[pl.BlockSpec((r, d), lambda l, n: (0, 0)),
                  pl.BlockSpec((None, d, tn), lambda l, n: (l, 0, n)),
                  pl.BlockSpec((None, 1, tn), lambda l, n: (l, 0, n))],
        out_specs=pl.BlockSpec((None, r, tn), lambda l, n: (l, 0, n)),
        compiler_params=_cparams(("arbitrary", "arbitrary")),
        name="ada_mod",
    )(cvecs, w_ada, b_ada.reshape(depth, 1, e))


def _in_kernel(x_ref, xp_ref, xn_ref, shift_ref, scale_ref, lng_ref, wqkv_ref, wbd_ref, wrest_ref,
               convw_ref, adt_ref, sgug_ref, wsp_ref, bsp_ref, cch_ref, mc_ref, ltri_ref, utri_ref,
               q_ref, k_ref, v_ref, bd_ref, sza_ref, osgu_ref, re_ref, im_ref, szc_ref, sg_ref,
               *, seq_tiles, grid_cols):
    i = pl.program_id(0)
    t_in_seq = i % seq_tiles
    lng = lng_ref[...]
    scale1 = 1.0 + scale_ref[...]
    shift = shift_ref[...]

    x = jnp.concatenate([xp_ref[...], x_ref[...], xn_ref[...]], axis=0)
    y = x * lax.rsqrt(jnp.mean(x * x, axis=-1, keepdims=True) + EPS)
    xe = ((y * lng) * scale1 + shift).astype(BF16)
    xn = xe[HALO:HALO + TM]

    pe = jnp.dot(xe, wqkv_ref[...], preferred_element_type=F32)
    p = pe[HALO:HALO + TM]
    p_prev = jnp.where(t_in_seq == 0, 0.0, pe[HALO - 1:HALO])
    p_next = jnp.where(t_in_seq == seq_tiles - 1, 0.0, pe[HALO + TM:HALO + TM + 1])
    row = lax.broadcasted_iota(jnp.int32, (TM, 1), 0)
    down = jnp.where(row == 0, p_prev, pltpu.roll(p, 1, axis=0))
    up = jnp.where(row == TM - 1, p_next, pltpu.roll(p, TM - 1, axis=0))
    cw = convw_ref[...]
    y = _silu(cw[0:1] * down + cw[1:2] * p + cw[2:3] * up)
    for h in range(H_A):
        qh = y[:, h * DK:(h + 1) * DK]
        kh = y[:, W_BR + h * DK:W_BR + (h + 1) * DK]
        q_ref[h] = qh * lax.rsqrt(jnp.sum(qh * qh, axis=-1, keepdims=True) + EPS) * (DK ** -0.5)
        k_ref[h] = kh * lax.rsqrt(jnp.sum(kh * kh, axis=-1, keepdims=True) + EPS)
        v_ref[h] = y[:, 2 * W_BR + h * DK:2 * W_BR + (h + 1) * DK]

    pbd = jnp.dot(xn, wbd_ref[...], preferred_element_type=F32)
    adt = adt_ref[...]
    sp_in = pbd + adt[1:2]
    softplus = jnp.maximum(sp_in, 0.0) + jnp.log1p(jnp.exp(-jnp.abs(sp_in)))
    g = -jnp.exp(adt[0:1]) * softplus
    lane = lax.broadcasted_iota(jnp.int32, (TM, 128), 1)
    g = jnp.where((lane >= 2 * H_A) & (lane < 4 * H_A), g, 0.0)
    g1 = g.astype(BF16).astype(F32)
    g2 = (g - g1).astype(BF16).astype(F32)
    g3 = (g - g1 - g2).astype(BF16).astype(F32)
    packed = (g1 + pltpu.roll(g2, 2 * H_A, axis=1) + pltpu.roll(g3, 4 * H_A, axis=1)).astype(BF16)

    def unpack(r):
        return r + pltpu.roll(r, 128 - 2 * H_A, axis=1) + pltpu.roll(r, 128 - 4 * H_A, axis=1)

    gpre = unpack(jnp.dot(ltri_ref[...], packed, preferred_element_type=F32))
    gsuf = unpack(jnp.dot(utri_ref[...], packed, preferred_element_type=F32))
    bd_ref[...] = jnp.where(lane < 2 * H_A, _sigmoid(pbd), jnp.where(lane < 3 * H_A, gpre, gsuf))

    def rest(j):
        return jnp.dot(xn, wrest_ref[:, j * W_BR:(j + 1) * W_BR], preferred_element_type=F32)

    sza_ref[...] = _silu(rest(0)).astype(BF16)

    u_b, v_b, z_b = rest(1), rest(2), rest(3)
    vn = (v_b * lax.rsqrt(jnp.mean(v_b * v_b, axis=-1, keepdims=True) + EPS) * sgug_ref[...]).astype(BF16)
    gate_b = _silu(z_b)
    for n in range(TM // CHUNK_B):
        rs = slice(n * CHUNK_B, (n + 1) * CHUNK_B)
        for gi in range(G_B):
            cs = slice(gi * (W_BR // G_B), (gi + 1) * (W_BR // G_B))
            vs = jnp.dot(wsp_ref[gi], vn[rs, cs], preferred_element_type=F32) + bsp_ref[gi]
            osgu_ref[rs, cs] = ((u_b[rs, cs] * vs) * gate_b[rs, cs]).astype(BF16)

    x_c = rest(4).astype(BF16)
    szc_ref[...] = _silu(rest(5)).astype(BF16)
    cch = cch_ref[...]
    a_parts, b_parts = [], []
    for gi in range(G_C):
        ab = jnp.dot(x_c[:, gi * CH_C:(gi + 1) * CH_C], cch, preferred_element_type=F32)
        a_parts.append(ab[:, :CH_C])
        b_parts.append(ab[:, CH_C:])
    a = jnp.concatenate(a_parts, axis=1)
    b = jnp.concatenate(b_parts, axis=1)
    if grid_cols:
        mc = mc_ref[...]
        for r in range(TM // GRID_W):
            rs = slice(r * GRID_W, (r + 1) * GRID_W)
            stacked = jnp.concatenate([a[rs], b[rs]], axis=0).astype(BF16)
            z = jnp.dot(mc, stacked, preferred_element_type=F32)
            re_ref[rs, :] = z[:GRID_W].astype(BF16)
            im_ref[rs, :] = z[GRID_W:].astype(BF16)
    else:
        re_ref[...] = a.astype(BF16)
        im_ref[...] = (-b).astype(BF16)

    for n in range(3):
        for j in range(2):
            c0 = 6 * W_BR + (2 * n + j) * W_BR
            sg_ref[n, :, j * W_BR:(j + 1) * W_BR] = _sigmoid(
                jnp.dot(xn, wrest_ref[:, c0:c0 + W_BR], preferred_element_type=F32)).astype(BF16)


def _in_call(h, shift, scale, row_of_tile, seq_tiles, grid_cols, l, wts, consts):
    ntok, d = h.shape
    nt = ntok // TM
    nhb = ntok // HALO
    tile = lambda w: pl.BlockSpec((TM, w), lambda i: (i, 0))
    mod = pl.BlockSpec((None, 1, d), lambda i: (row_of_tile(i), 0, 0))
    lws = [wts[n] for n in ("ln_g", "w_qkv", "w_bd", "w_rest", "conv_w", "adt", "sgu_g", "w_sp", "b_sp")]
    cs = [consts[n] for n in ("cch", "mc", "ltri", "utri")]
    in_specs = [tile(d),
                pl.BlockSpec((HALO, d), lambda i: (jnp.maximum(i * (TM // HALO) - 1, 0), 0)),
                pl.BlockSpec((HALO, d), lambda i: (jnp.minimum((i + 1) * (TM // HALO), nhb - 1), 0)),
                mod, mod] + [_layer_spec(a, l) for a in lws] + [_full_spec(a) for a in cs]
    hsplit = pl.BlockSpec((H_A, TM, DK), lambda i: (0, i, 0))
    out_shape = [jax.ShapeDtypeStruct((H_A, ntok, DK), F32)] * 3 + [
        jax.ShapeDtypeStruct((ntok, 128), F32),
        jax.ShapeDtypeStruct((ntok, W_BR), BF16),
        jax.ShapeDtypeStruct((ntok, W_BR), BF16),
        jax.ShapeDtypeStruct((ntok, W_BR), BF16),
        jax.ShapeDtypeStruct((ntok, W_BR), BF16),
        jax.ShapeDtypeStruct((ntok, W_BR), BF16),
        jax.ShapeDtypeStruct((3, ntok, 2 * W_BR), BF16),
    ]
    out_specs = [hsplit] * 3 + [tile(128), tile(W_BR), tile(W_BR), tile(W_BR), tile(W_BR), tile(W_BR),
                                pl.BlockSpec((3, TM, 2 * W_BR), lambda i: (0, i, 0))]
    return pl.pallas_call(
        functools.partial(_in_kernel, seq_tiles=seq_tiles, grid_cols=grid_cols),
        out_shape=out_shape, grid=(nt,), in_specs=in_specs, out_specs=out_specs,
        compiler_params=_cparams(("arbitrary",)), name="in_proj",
    )(h, h, h, shift, scale, *lws, *cs)


def _dot3(x, y):
    xh, xl = _split(x)
    yh, yl = _split(y)
    n = y.shape[1]
    rhs = jnp.concatenate([jnp.concatenate([yh, yl], axis=1),
                           jnp.concatenate([yh, jnp.zeros_like(yh)], axis=1)], axis=0)
    z = jnp.dot(jnp.concatenate([xh, xl], axis=1), rhs, preferred_element_type=F32)
    return z[:, :n] + z[:, n:]


def _delta_kernel(qf_ref, kf_ref, vf_ref, bdf_ref, bdtf_ref, qb_ref, kb_ref, vb_ref, bdb_ref, bdtb_ref, s0_ref,
                  of_ref, ob_ref, s_ref):
    j = pl.program_id(1)

    @pl.when(j == 0)
    def _():
        s_ref[...] = s0_ref[...]

    nch = qf_ref.shape[1] // CHUNK
    ri = lax.broadcasted_iota(jnp.int32, (PAIR, PAIR), 0)
    ci = lax.broadcasted_iota(jnp.int32, (PAIR, PAIR), 1)
    same = (ri < CHUNK) == (ci < CHUNK)
    eye = (ri == ci).astype(F32)
    masks = (same & (ri > ci), same & (ri < ci))
    top = lax.broadcasted_iota(jnp.int32, (PAIR, 1), 0) < CHUNK
    refs = ((qf_ref, kf_ref, vf_ref, bdf_ref, bdtf_ref, of_ref), (qb_ref, kb_ref, vb_ref, bdb_ref, bdtb_ref, ob_ref))
    tr_rhs = (((1,), (1,)), ((), ()))
    tr_lhs = (((0,), (0,)), ((), ()))
    dps = [(d, p) for d in range(2) for p in range(H_A // 2)]

    def blockdiag(x):
        return jnp.concatenate([jnp.where(top, x, 0.0), jnp.where(top, 0.0, x)], axis=1)

    def local(steps):
        probs = [(s, d, p) for s in steps for d, p in dps]
        n = len(probs)
        ld = []
        for s, d, p in probs:
            q_ref, k_ref, v_ref, bd_ref, bdt_ref, _ = refs[d]
            c = s if d == 0 else nch - 1 - s
            sl = slice(c * CHUNK, (c + 1) * CHUNK)
            pair = lambda r: jnp.concatenate([r[2 * p, sl, :], r[2 * p + 1, sl, :]], axis=0)
            col = d * H_A + 2 * p
            colpair = lambda c0: jnp.concatenate([bd_ref[sl, c0:c0 + 1], bd_ref[sl, c0 + 1:c0 + 2]], axis=0)
            gam = colpair(2 * H_A + col)
            last = CHUNK - 1 if d == 0 else 0
            tots = [gam[e * CHUNK + last:e * CHUNK + last + 1] for e in range(2)]
            ld.append(dict(q=pair(q_ref), k=pair(k_ref), v=pair(v_ref), beta=colpair(col), gam=gam,
                           gam_row=bdt_ref[c, 2 * d + p:2 * d + p + 1, :], tots=tots, sl=sl))
        yield
        qkk, dec = [], []
        for i, (s, d, p) in enumerate(probs):
            x = ld[i]
            kb = x["k"].astype(BF16)
            qkk.append(lax.dot_general(jnp.concatenate([x["q"].astype(BF16), kb], axis=0), kb, tr_rhs,
                                       preferred_element_type=F32))
            m = masks[d]
            dec.append(jnp.where(m, jnp.exp(jnp.where(m, x["gam"] - x["gam_row"], 0.0)), 0.0))
        yield
        pw = [-((ld[i]["beta"] * qkk[i][PAIR:]) * dec[i]) for i in range(n)]
        t = [eye + pw[i] for i in range(n)]
        pw = [_dot3(pw[i], pw[i]) for i in range(n)]
        yield
        for _ in range(4):
            tp = [_dot3(jnp.concatenate([t[i], pw[i]], axis=0), pw[i]) for i in range(n)]
            t = [t[i] + tp[i][:PAIR] for i in range(n)]
            pw = [tp[i][PAIR:] for i in range(n)]
            yield
        t = [t[i] + _dot3(t[i], pw[i]) for i in range(n)]
        yield
        eg = [jnp.exp(ld[i]["gam"]) for i in range(n)]
        sol = [_dot3(t[i], jnp.concatenate([ld[i]["v"] * ld[i]["beta"], ld[i]["k"] * (ld[i]["beta"] * eg[i])], axis=1))
               for i in range(n)]
        yield
        out = {}
        for i, (s, d, p) in enumerate(probs):
            x = ld[i]
            tot = jnp.concatenate([jnp.broadcast_to(tt, (CHUNK, 1)) for tt in x["tots"]], axis=0)
            out[s, d, p] = dict(
                lhs=jnp.concatenate([blockdiag(sol[i][:, DK:]), blockdiag(x["q"] * eg[i])], axis=0).astype(BF16),
                u0=sol[i][:, :DK],
                qk=(qkk[i][:PAIR] * (dec[i] + eye)).astype(BF16),
                kd=blockdiag(x["k"] * jnp.exp(tot - x["gam"])).astype(BF16),
                gl=jnp.concatenate([jnp.broadcast_to(jnp.exp(tt), (DK, 1)) for tt in x["tots"]], axis=0),
                sl=x["sl"])
        return out

    st = {dp: s_ref[dp[0], 2 * dp[1]:2 * dp[1] + 2].reshape(2 * DK, DK) for dp in dps}

    def scan(loc, steps):
        for s in steps:
            wq = {dp: jnp.dot(loc[(s,) + dp]["lhs"], st[dp].astype(BF16), preferred_element_type=F32) for dp in dps}
            yield
            for d, p in dps:
                x = loc[s, d, p]
                ub = (x["u0"] - wq[d, p][:PAIR]).astype(BF16)
                o = wq[d, p][PAIR:] + jnp.dot(x["qk"], ub, preferred_element_type=F32)
                st[d, p] = x["gl"] * st[d, p] + lax.dot_general(x["kd"], ub, tr_lhs, preferred_element_type=F32)
                for e in range(2):
                    hh = 2 * p + e
                    refs[d][5][x["sl"], hh * DK:(hh + 1) * DK] = o[e * CHUNK:(e + 1) * CHUNK]
            yield

    def run(gen, filler):
        while True:
            try:
                next(gen)
            except StopIteration as done:
                return done.value
            if filler is not None:
                next(filler, None)

    groups = [list(range(g, min(g + LOCKSTEP, nch))) for g in range(0, nch, LOCKSTEP)]
    filler = None
    for steps in groups:
        loc = run(local(steps), filler)
        if filler is not None:
            for _ in filler:
                pass
        filler = scan(loc, steps)
    for _ in filler:
        pass
    for d, p in dps:
        s_ref[d, 2 * p:2 * p + 2] = st[d, p].reshape(2, DK, DK)


def _delta_call(q, k, v, bd, bdt, s0, l, seq):
    ntok = bd.shape[0]
    td = min(TD, seq)
    nb = seq // td
    bsz = ntok // seq
    nch = td // CHUNK
    fw = lambda b, j: b * nb + j
    bw = lambda b, j: b * nb + nb - 1 - j

    def specs(t):
        hs = pl.BlockSpec((H_A, td, DK), lambda b, j: (0, t(b, j), 0))
        return [hs, hs, hs, pl.BlockSpec((td, 128), lambda b, j: (t(b, j), 0)),
                pl.BlockSpec((nch, H_A, PAIR), lambda b, j: (t(b, j), 0, 0))]

    return pl.pallas_call(
        _delta_kernel,
        out_shape=[jax.ShapeDtypeStruct((ntok, W_BR), F32), jax.ShapeDtypeStruct((ntok, W_BR), F32),
                   jax.ShapeDtypeStruct((bsz, 2, H_A, DK, DK), F32)],
        grid=(bsz, nb),
        in_specs=specs(fw) + specs(bw) + [pl.BlockSpec((None, None, 2, H_A, DK, DK),
                                                       lambda b, j: (b, l, 0, 0, 0, 0))],
        out_specs=[pl.BlockSpec((td, W_BR), lambda b, j: (fw(b, j), 0)),
                   pl.BlockSpec((td, W_BR), lambda b, j: (bw(b, j), 0)),
                   pl.BlockSpec((None, 2, H_A, DK, DK), lambda b, j: (b, 0, 0, 0, 0))],
        compiler_params=_cparams(("arbitrary", "arbitrary")), name="delta_scan",
    )(q, k, v, bd, bdt, q, k, v, bd, bdt, s0)


def _fourier_kernel(cr_ref, sr_ref, re_ref, im_ref, szc_ref, o_ref):
    y = (jnp.dot(cr_ref[...], re_ref[...], preferred_element_type=F32)
         + jnp.dot(sr_ref[...], im_ref[...], preferred_element_type=F32))
    o_ref[...] = (y * szc_ref[...]).astype(BF16)


def _fourier_call(cr, sr, re, im, szc, tn):
    bsz, r, n = re.shape
    blk = pl.BlockSpec((None, r, tn), lambda b, c: (b, 0, c))
    mat = pl.BlockSpec((r, r), lambda b, c: (0, 0))
    return pl.pallas_call(
        _fourier_kernel, out_shape=jax.ShapeDtypeStruct((bsz, r, n), BF16),
        grid=(bsz, n // tn), in_specs=[mat, mat, blk, blk, blk], out_specs=blk,
        compiler_params=_cparams(("arbitrary", "arbitrary")), name="fourier_rows",
    )(cr, sr, re, im, szc)


def _fourier_grid_kernel(cr_ref, sr_ref, re_ref, im_ref, szc_ref, o_ref):
    r, s, w = re_ref.shape
    y = (jnp.dot(cr_ref[...], re_ref[...].reshape(r * s, w), preferred_element_type=F32)
         + jnp.dot(sr_ref[...], im_ref[...].reshape(r * s, w), preferred_element_type=F32))
    o_ref[...] = (y * szc_ref[...].reshape(r * s, w)).astype(BF16).reshape(r, s, w)


def _fourier_grid_call(crk, srk, re, im, szc):
    bsz, r, gw, w = re.shape
    blk = pl.BlockSpec((None, r, COLS_SLAB, w), lambda b, c: (b, 0, c, 0))
    mat = pl.BlockSpec(crk.shape, lambda b, c: (0, 0))
    return pl.pallas_call(
        _fourier_grid_kernel, out_shape=jax.ShapeDtypeStruct((bsz, r, gw, w), BF16),
        grid=(bsz, gw // COLS_SLAB), in_specs=[mat, mat, blk, blk, blk], out_specs=blk,
        compiler_params=_cparams(("arbitrary", "arbitrary")), name="fourier_grid_rows",
    )(crk, srk, re, im, szc)


def _merge_kernel(h_ref, of_ref, ob_ref, sza_ref, osgu_ref, oc_ref, sg_ref, gate_ref, ong_ref, wbr_ref, wout_ref,
                  fing_ref, o_ref, *, final):
    o = of_ref[...] + ob_ref[...]
    sza = sza_ref[...]
    ong = ong_ref[...]
    parts = []
    for hh in range(H_A):
        oh = o[:, hh * DK:(hh + 1) * DK]
        oh = (oh * lax.rsqrt(jnp.mean(oh * oh, axis=-1, keepdims=True) + EPS)) * ong
        parts.append((oh * sza[:, hh * DK:(hh + 1) * DK]).astype(BF16))
    o_a = jnp.concatenate(parts, axis=1)
    merged = (sg_ref[0] * jnp.dot(o_a, wbr_ref[0], preferred_element_type=F32)
              + sg_ref[1] * jnp.dot(osgu_ref[...], wbr_ref[1], preferred_element_type=F32)
              + sg_ref[2] * jnp.dot(oc_ref[...], wbr_ref[2], preferred_element_type=F32))
    out = jnp.dot(merged.astype(BF16), wout_ref[...], preferred_element_type=F32)
    hn = h_ref[...] + gate_ref[...] * out
    if final:
        hn = (hn * lax.rsqrt(jnp.mean(hn * hn, axis=-1, keepdims=True) + EPS)) * fing_ref[...]
    o_ref[...] = hn


def _merge_call(h, o_f, o_b, sza, osgu, oc, sg, gate, row_of_tile, l, wts, final_g, final, tm):
    ntok, d = h.shape
    tile = lambda w: pl.BlockSpec((tm, w), lambda i: (i, 0))
    lws = [wts[n] for n in ("o_norm_g", "w_branch", "w_out")]
    in_specs = [tile(d), tile(W_BR), tile(W_BR), tile(W_BR), tile(W_BR), tile(W_BR),
                pl.BlockSpec((3, tm, d), lambda i: (0, i, 0)),
                pl.BlockSpec((None, 1, d), lambda i: (row_of_tile(i * (tm // TM)), 0, 0))
                ] + [_layer_spec(a, l) for a in lws] + [_full_spec(final_g)]
    return pl.pallas_call(
        functools.partial(_merge_kernel, final=final),
        out_shape=jax.ShapeDtypeStruct((ntok, d), F32), grid=(ntok // tm,),
        in_specs=in_specs, out_specs=tile(d),
        compiler_params=_cparams(("arbitrary",)), name="merge_out",
    )(h, o_f, o_b, sza, osgu, oc, sg, gate, *lws, final_g)


def _dft(n):
    idx = np.arange(n)
    ang = 2.0 * np.pi * ((idx[:, None] * idx[None, :]) % n) / n
    return np.cos(ang) / np.sqrt(n), np.sin(ang) / np.sqrt(n)


def _consts(seq, rows):
    cc, sc = _dft(CH_C)
    cg, sg = _dft(GRID_W)
    cp, sp = _dft(seq)
    cr, sr = _dft(rows)
    blk = np.kron(np.eye(TM // CHUNK), np.tril(np.ones((CHUNK, CHUNK))))
    bf = lambda a: jnp.asarray(a, F32).astype(BF16)
    return {
        "cch": bf(np.concatenate([cc, sc], axis=1)),
        "mc": bf(np.block([[cg, -sg], [-sg, -cg]])),
        "ltri": bf(blk), "utri": bf(blk.T),
        "cp": bf(cp), "sp": bf(sp),
        "crk": bf(np.kron(cr, np.eye(COLS_SLAB))), "srk": bf(np.kron(sr, np.eye(COLS_SLAB))),
    }


def _stacked_weights(w_in, ln_g, conv_w, a_log, dt_bias, o_norm_g, sgu_norm_g, w_spatial, b_spatial, w_branch, w_out):
    depth, d, _ = w_in.shape
    nq = 3 * W_BR
    nbd = 4 * H_A
    adt = jnp.stack([a_log.reshape(depth, -1), dt_bias.reshape(depth, -1)], axis=1)
    return {
        "ln_g": ln_g.reshape(depth, 1, d),
        "w_qkv": w_in[:, :, :nq].astype(BF16),
        "w_bd": jnp.pad(w_in[:, :, nq:nq + nbd], ((0, 0), (0, 0), (0, 128 - nbd))).astype(BF16),
        "w_rest": w_in[:, :, nq + nbd:].astype(BF16),
        "conv_w": conv_w,
        "adt": jnp.pad(adt, ((0, 0), (0, 0), (2 * H_A, 128 - nbd))),
        "sgu_g": sgu_norm_g.reshape(depth, 1, W_BR),
        "w_sp": w_spatial.astype(BF16),
        "b_sp": jnp.broadcast_to(b_spatial[..., None], b_spatial.shape + (W_BR // G_B,)),
        "o_norm_g": o_norm_g.reshape(depth, 1, DK),
        "w_branch": w_branch.astype(BF16),
        "w_out": w_out.astype(BF16),
    }


def _group_forward(x, mods, row_of_tile, s0_all, grid_cols, wts, consts, final_g):
    bsz, t, d = x.shape
    ntok = bsz * t
    nb = t // TM
    h = x.reshape(ntok, d)
    tm_merge = TMM if (t % TMM == 0 or not grid_cols) and ntok % TMM == 0 else TM
    states = []
    depth = mods.shape[0]
    for l in range(depth):
        shift, scale, gate = mods[l, 0], mods[l, 1], mods[l, 2]
        q, k, v, bd, sza, osgu, re, im, szc, sg = _in_call(h, shift, scale, row_of_tile, nb, grid_cols, l, wts, consts)
        bdt = bd[:, 2 * H_A:4 * H_A].reshape(ntok // CHUNK, CHUNK, H_A, 2).transpose(0, 2, 3, 1).reshape(
            ntok // CHUNK, H_A, PAIR)
        o_f, o_b, s_fin = _delta_call(q, k, v, bd, bdt, s0_all, min(l, s0_all.shape[1] - 1), t)
        states.append(s_fin)
        if grid_cols:
            shp = (bsz, t // GRID_W, GRID_W, W_BR)
            oc = _fourier_grid_call(consts["crk"], consts["srk"], re.reshape(shp), im.reshape(shp), szc.reshape(shp))
        else:
            shp = (bsz, t, W_BR)
            oc = _fourier_call(consts["cp"], consts["sp"], re.reshape(shp), im.reshape(shp), szc.reshape(shp), W_BR)
        h = _merge_call(h, o_f, o_b, sza, osgu, oc.reshape(ntok, W_BR), sg, gate, row_of_tile, l, wts,
                        final_g.reshape(1, d), l == depth - 1, tm_merge)
    return h.reshape(bsz, t, d), states


def kernel(x_prompt, x_sample, state_delta, c, c_ctx, ln_g, w_ada, b_ada, w_in, conv_w, a_log, dt_bias, o_norm_g,
           sgu_norm_g, w_spatial, b_spatial, w_branch, w_out, final_g):
    depth, d = ln_g.shape
    bp, tp, _ = x_prompt.shape
    bs, ts, _ = x_sample.shape
    consts = _consts(tp, ts // GRID_W)
    wts = _stacked_weights(w_in, ln_g, conv_w, a_log, dt_bias, o_norm_g, sgu_norm_g, w_spatial, b_spatial,
                           w_branch, w_out)
    nrow = -(-(1 + bs) // 8) * 8
    cvecs = jnp.concatenate([c_ctx[None], c, jnp.zeros((nrow - 1 - bs, d), F32)], axis=0)
    mods = _ada_call(cvecs, w_ada, b_ada).reshape(depth, nrow, 3, 1, d).transpose(0, 2, 1, 3, 4)

    zeros = jnp.zeros((bp, 1, 2, H_A, DK, DK), F32)
    y_prompt, st = _group_forward(x_prompt, mods, lambda i: 0, zeros, False, wts, consts, final_g)
    tiles_s = ts // TM
    y_sample, _ = _group_forward(x_sample, mods, lambda i: 1 + i // tiles_s, state_delta, True, wts, consts, final_g)
    return y_prompt, y_sample, jnp.stack(st, axis=1)
```

```python
import functools

import numpy as np
import jax
import jax.numpy as jnp
from jax import lax
from jax.experimental import pallas as pl
from jax.experimental.pallas import tpu as pltpu

F32 = jnp.float32
BF16 = jnp.bfloat16

W_BR = 512
H_A = 4
DK = 128
CONV_K = 3
CHUNK = 64
PAIR = 2 * CHUNK
G_B = 4
CHUNK_B = 128
G_C = 4
CH_C = W_BR // G_C
GRID_W = 64
COLS_SLAB = 16
EPS = 1e-6
C_BD = 3 * W_BR
C_REST = C_BD + 128
TM = 256
TMM = 512
TD = 512
LOCKSTEP = 2
HALO = 16
VMEM_LIMIT = 56 * 1024 * 1024


def _bdot(a, b):
    return jnp.dot(a.astype(BF16), b.astype(BF16), preferred_element_type=F32)


def _split(x):
    hi = x.astype(BF16)
    lo = (x - hi.astype(F32)).astype(BF16)
    return hi, lo


def _sigmoid(x):
    return 0.5 * jnp.tanh(0.5 * x) + 0.5


def _silu(x):
    return x * _sigmoid(x)


def _cparams(sem):
    return pltpu.CompilerParams(dimension_semantics=sem, vmem_limit_bytes=VMEM_LIMIT)


def _layer_spec(a, l):
    return pl.BlockSpec((None,) + a.shape[1:], lambda *_: (l,) + (0,) * (a.ndim - 1))


def _full_spec(a):
    return pl.BlockSpec(a.shape, lambda *_: (0,) * a.ndim)


def _ada_kernel(c_ref, w_ref, b_ref, o_ref):
    o_ref[...] = _bdot(_silu(c_ref[...]), w_ref[...]) + b_ref[...]


def _ada_call(cvecs, w_ada, b_ada):
    depth, d, e = w_ada.shape
    r = cvecs.shape[0]
    tn = 1024
    return pl.pallas_call(
        _ada_kernel,
        out_shape=jax.ShapeDtypeStruct((depth, r, e), F32),
        grid=(depth, e // tn),
        in_specs=[pl.BlockSpec((r, d), lambda l, n: (0, 0)),
                  pl.BlockSpec((None, d, tn), lambda l, n: (l, 0, n)),
                  pl.BlockSpec((None, 1, tn), lambda l, n: (l, 0, n))],
        out_specs=pl.BlockSpec((None, r, tn), lambda l, n: (l, 0, n)),
        compiler_params=_cparams(("arbitrary", "arbitrary")),
        name="ada_mod",
    )(cvecs, w_ada, b_ada.reshape(depth, 1, e))


def _in_kernel(x_ref, xp_ref, xn_ref, shift_ref, scale_ref, lng_ref, win_ref,
               convw_ref, adt_ref, sgug_ref, wsp_ref, bsp_ref, cch_ref, mc_ref, ltri_ref, utri_ref,
               q_ref, k_ref, v_ref, bd_ref, sza_ref, osgu_ref, re_ref, im_ref, szc_ref, sg_ref,
               *, seq_tiles, grid_cols):
    i = pl.program_id(0)
    t_in_seq = i % seq_tiles
    lng = lng_ref[...]
    scale1 = 1.0 + scale_ref[...]
    shift = shift_ref[...]

    x = jnp.concatenate([xp_ref[...], x_ref[...], xn_ref[...]], axis=0)
    y = x * lax.rsqrt(jnp.mean(x * x, axis=-1, keepdims=True) + EPS)
    xe = ((y * lng) * scale1 + shift).astype(BF16)
    xn = xe[HALO:HALO + TM]

    def rest(j):
        return jnp.dot(xn, win_ref[:, C_REST + j * W_BR:C_REST + (j + 1) * W_BR], preferred_element_type=F32)

    pe = jnp.dot(xe, win_ref[:, :C_BD], preferred_element_type=F32)
    pbd = jnp.dot(xn, win_ref[:, C_BD:C_REST], preferred_element_type=F32)
    z_a = rest(0)

    p_prev = jnp.where(t_in_seq == 0, 0.0, pe[HALO - 1:HALO])
    p_next = jnp.where(t_in_seq == seq_tiles - 1, 0.0, pe[HALO + TM:HALO + TM + 1])
    row = lax.broadcasted_iota(jnp.int32, (TM, 1), 0)
    cw = convw_ref[...]

    def conv_silu(c0):
        cs = slice(c0, c0 + DK)
        p = pe[HALO:HALO + TM, cs]
        down = jnp.where(row == 0, p_prev[:, cs], pltpu.roll(p, 1, axis=0))
        up = jnp.where(row == TM - 1, p_next[:, cs], pltpu.roll(p, TM - 1, axis=0))
        return _silu(cw[0:1, cs] * down + cw[1:2, cs] * p + cw[2:3, cs] * up)

    def qkv_head(h):
        qh = conv_silu(h * DK)
        kh = conv_silu(W_BR + h * DK)
        q_ref[h] = qh * lax.rsqrt(jnp.sum(qh * qh, axis=-1, keepdims=True) + EPS) * (DK ** -0.5)
        k_ref[h] = kh * lax.rsqrt(jnp.sum(kh * kh, axis=-1, keepdims=True) + EPS)
        v_ref[h] = conv_silu(2 * W_BR + h * DK)

    def gate_piece(m):
        return jnp.dot(xn, win_ref[:, C_REST + (6 + m) * W_BR:C_REST + (7 + m) * W_BR], preferred_element_type=F32)

    def store_gate(m, gp):
        sg_ref[m // 2, :, (m % 2) * W_BR:(m % 2 + 1) * W_BR] = _sigmoid(gp).astype(BF16)

    u_b = rest(1)
    qkv_head(0)
    v_b = rest(2)
    qkv_head(1)
    z_b = rest(3)
    qkv_head(2)
    x_c = rest(4).astype(BF16)
    qkv_head(3)
    z_c = rest(5)

    adt = adt_ref[...]
    sp_in = pbd + adt[1:2]
    softplus = jnp.maximum(sp_in, 0.0) + jnp.log1p(jnp.exp(-jnp.abs(sp_in)))
    g = -jnp.exp(adt[0:1]) * softplus
    lane = lax.broadcasted_iota(jnp.int32, (TM, 128), 1)
    g = jnp.where((lane >= 2 * H_A) & (lane < 4 * H_A), g, 0.0)
    g1 = g.astype(BF16).astype(F32)
    g2 = (g - g1).astype(BF16).astype(F32)
    g3 = (g - g1 - g2).astype(BF16).astype(F32)
    packed = (g1 + pltpu.roll(g2, 2 * H_A, axis=1) + pltpu.roll(g3, 4 * H_A, axis=1)).astype(BF16)

    def unpack(r):
        return r + pltpu.roll(r, 128 - 2 * H_A, axis=1) + pltpu.roll(r, 128 - 4 * H_A, axis=1)

    gpre = unpack(jnp.dot(ltri_ref[...], packed, preferred_element_type=F32))
    gsuf = unpack(jnp.dot(utri_ref[...], packed, preferred_element_type=F32))
    bd_ref[...] = jnp.where(lane < 2 * H_A, _sigmoid(pbd), jnp.where(lane < 3 * H_A, gpre, gsuf))
    sza_ref[...] = _silu(z_a).astype(BF16)

    vn = (v_b * lax.rsqrt(jnp.mean(v_b * v_b, axis=-1, keepdims=True) + EPS) * sgug_ref[...]).astype(BF16)
    gate_b = _silu(z_b)

    def sgu_chunk(n):
        rs = slice(n * CHUNK_B, (n + 1) * CHUNK_B)
        for gi in range(G_B):
            cs = slice(gi * (W_BR // G_B), (gi + 1) * (W_BR // G_B))
            vs = jnp.dot(wsp_ref[gi], vn[rs, cs], preferred_element_type=F32) + bsp_ref[gi]
            osgu_ref[rs, cs] = ((u_b[rs, cs] * vs) * gate_b[rs, cs]).astype(BF16)

    gps = [gate_piece(0)]
    for n in range(TM // CHUNK_B):
        sgu_chunk(n)
        gps.append(gate_piece(len(gps)))

    szc_ref[...] = _silu(z_c).astype(BF16)
    cch = cch_ref[...]
    a_parts, b_parts = [], []
    for gi in range(G_C):
        ab = jnp.dot(x_c[:, gi * CH_C:(gi + 1) * CH_C], cch, preferred_element_type=F32)
        a_parts.append(ab[:, :CH_C])
        b_parts.append(ab[:, CH_C:])
    a = jnp.concatenate(a_parts, axis=1)
    b = jnp.concatenate(b_parts, axis=1)
    gps.append(gate_piece(len(gps)))
    if grid_cols:
        mc = mc_ref[...]
        for r in range(TM // GRID_W):
            rs = slice(r * GRID_W, (r + 1) * GRID_W)
            stacked = jnp.concatenate([a[rs], b[rs]], axis=0).astype(BF16)
            z = jnp.dot(mc, stacked, preferred_element_type=F32)
            re_ref[rs, :] = z[:GRID_W].astype(BF16)
            im_ref[rs, :] = z[GRID_W:].astype(BF16)
    else:
        re_ref[...] = a.astype(BF16)
        im_ref[...] = (-b).astype(BF16)

    for m in range(6):
        if len(gps) < 6:
            gps.append(gate_piece(len(gps)))
        store_gate(m, gps[m])


def _in_call(h, shift, scale, row_of_tile, seq_tiles, grid_cols, l, wts, consts):
    ntok, d = h.shape
    nt = ntok // TM
    nhb = ntok // HALO
    tile = lambda w: pl.BlockSpec((TM, w), lambda i: (i, 0))
    mod = pl.BlockSpec((None, 1, d), lambda i: (row_of_tile(i), 0, 0))
    lws = [wts[n] for n in ("ln_g", "w_in", "conv_w", "adt", "sgu_g", "w_sp", "b_sp")]
    cs = [consts[n] for n in ("cch", "mc", "ltri", "utri")]
    in_specs = [tile(d),
                pl.BlockSpec((HALO, d), lambda i: (jnp.maximum(i * (TM // HALO) - 1, 0), 0)),
                pl.BlockSpec((HALO, d), lambda i: (jnp.minimum((i + 1) * (TM // HALO), nhb - 1), 0)),
                mod, mod] + [_layer_spec(a, l) for a in lws] + [_full_spec(a) for a in cs]
    hsplit = pl.BlockSpec((H_A, TM, DK), lambda i: (0, i, 0))
    out_shape = [jax.ShapeDtypeStruct((H_A, ntok, DK), F32)] * 3 + [
        jax.ShapeDtypeStruct((ntok, 128), F32),
        jax.ShapeDtypeStruct((ntok, W_BR), BF16),
        jax.ShapeDtypeStruct((ntok, W_BR), BF16),
        jax.ShapeDtypeStruct((ntok, W_BR), BF16),
        jax.ShapeDtypeStruct((ntok, W_BR), BF16),
        jax.ShapeDtypeStruct((ntok, W_BR), BF16),
        jax.ShapeDtypeStruct((3, ntok, 2 * W_BR), BF16),
    ]
    out_specs = [hsplit] * 3 + [tile(128), tile(W_BR), tile(W_BR), tile(W_BR), tile(W_BR), tile(W_BR),
                                pl.BlockSpec((3, TM, 2 * W_BR), lambda i: (0, i, 0))]
    return pl.pallas_call(
        functools.partial(_in_kernel, seq_tiles=seq_tiles, grid_cols=grid_cols),
        out_shape=out_shape, grid=(nt,), in_specs=in_specs, out_specs=out_specs,
        compiler_params=_cparams(("arbitrary",)), name="in_proj",
    )(h, h, h, shift, scale, *lws, *cs)


def _dot3(x, y):
    xh, xl = _split(x)
    yh, yl = _split(y)
    n = y.shape[1]
    rhs = jnp.concatenate([jnp.concatenate([yh, yl], axis=1),
                           jnp.concatenate([yh, jnp.zeros_like(yh)], axis=1)], axis=0)
    z = jnp.dot(jnp.concatenate([xh, xl], axis=1), rhs, preferred_element_type=F32)
    return z[:, :n] + z[:, n:]


def _delta_kernel(qf_ref, kf_ref, vf_ref, bdf_ref, bdtf_ref, qb_ref, kb_ref, vb_ref, bdb_ref, bdtb_ref, s0_ref,
                  of_ref, ob_ref, s_ref):
    j = pl.program_id(1)

    @pl.when(j == 0)
    def _():
        s_ref[...] = s0_ref[...]

    nch = qf_ref.shape[1] // CHUNK
    ri = lax.broadcasted_iota(jnp.int32, (PAIR, PAIR), 0)
    ci = lax.broadcasted_iota(jnp.int32, (PAIR, PAIR), 1)
    same = (ri < CHUNK) == (ci < CHUNK)
    eye = (ri == ci).astype(F32)
    masks = (same & (ri > ci), same & (ri < ci))
    top = lax.broadcasted_iota(jnp.int32, (PAIR, 1), 0) < CHUNK
    refs = ((qf_ref, kf_ref, vf_ref, bdf_ref, bdtf_ref, of_ref), (qb_ref, kb_ref, vb_ref, bdb_ref, bdtb_ref, ob_ref))
    tr_rhs = (((1,), (1,)), ((), ()))
    tr_lhs = (((0,), (0,)), ((), ()))
    dps = [(d, p) for d in range(2) for p in range(H_A // 2)]

    def blockdiag(x):
        return jnp.concatenate([jnp.where(top, x, 0.0), jnp.where(top, 0.0, x)], axis=1)

    def local(steps):
        probs = [(s, d, p) for s in steps for d, p in dps]
        n = len(probs)
        ld = []
        for s, d, p in probs:
            q_ref, k_ref, v_ref, bd_ref, bdt_ref, _ = refs[d]
            c = s if d == 0 else nch - 1 - s
            sl = slice(c * CHUNK, (c + 1) * CHUNK)
            pair = lambda r: jnp.concatenate([r[2 * p, sl, :], r[2 * p + 1, sl, :]], axis=0)
            col = d * H_A + 2 * p
            colpair = lambda c0: jnp.concatenate([bd_ref[sl, c0:c0 + 1], bd_ref[sl, c0 + 1:c0 + 2]], axis=0)
            gam = colpair(2 * H_A + col)
            last = CHUNK - 1 if d == 0 else 0
            tots = [gam[e * CHUNK + last:e * CHUNK + last + 1] for e in range(2)]
            ld.append(dict(q=pair(q_ref), k=pair(k_ref), v=pair(v_ref), beta=colpair(col), gam=gam,
                           gam_row=bdt_ref[c, 2 * d + p:2 * d + p + 1, :], tots=tots, sl=sl))
        yield
        qkk, dec = [], []
        for i, (s, d, p) in enumerate(probs):
            x = ld[i]
            kb = x["k"].astype(BF16)
            qkk.append(lax.dot_general(jnp.concatenate([x["q"].astype(BF16), kb], axis=0), kb, tr_rhs,
                                       preferred_element_type=F32))
            m = masks[d]
            dec.append(jnp.where(m, jnp.exp(jnp.where(m, x["gam"] - x["gam_row"], 0.0)), 0.0))
        yield
        pw = [-((ld[i]["beta"] * qkk[i][PAIR:]) * dec[i]) for i in range(n)]
        t = [eye + pw[i] for i in range(n)]
        pw = [_dot3(pw[i], pw[i]) for i in range(n)]
        yield
        for _ in range(4):
            tp = [_dot3(jnp.concatenate([t[i], pw[i]], axis=0), pw[i]) for i in range(n)]
            t = [t[i] + tp[i][:PAIR] for i in range(n)]
            pw = [tp[i][PAIR:] for i in range(n)]
            yield
        t = [t[i] + _dot3(t[i], pw[i]) for i in range(n)]
        yield
        eg = [jnp.exp(ld[i]["gam"]) for i in range(n)]
        sol = [_dot3(t[i], jnp.concatenate([ld[i]["v"] * ld[i]["beta"], ld[i]["k"] * (ld[i]["beta"] * eg[i])], axis=1))
               for i in range(n)]
        yield
        out = {}
        for i, (s, d, p) in enumerate(probs):
            x = ld[i]
            tot = jnp.concatenate([jnp.broadcast_to(tt, (CHUNK, 1)) for tt in x["tots"]], axis=0)
            out[s, d, p] = dict(
                lhs=jnp.concatenate([blockdiag(sol[i][:, DK:]), blockdiag(x["q"] * eg[i])], axis=0).astype(BF16),
                u0=sol[i][:, :DK],
                qk=(qkk[i][:PAIR] * (dec[i] + eye)).astype(BF16),
                kd=blockdiag(x["k"] * jnp.exp(tot - x["gam"])).astype(BF16),
                gl=jnp.concatenate([jnp.broadcast_to(jnp.exp(tt), (DK, 1)) for tt in x["tots"]], axis=0),
                sl=x["sl"])
        return out

    st = {dp: s_ref[dp[0], 2 * dp[1]:2 * dp[1] + 2].reshape(2 * DK, DK) for dp in dps}

    def scan(loc, steps):
        for s in steps:
            wq = {dp: jnp.dot(loc[(s,) + dp]["lhs"], st[dp].astype(BF16), preferred_element_type=F32) for dp in dps}
            yield
            for d, p in dps:
                x = loc[s, d, p]
                ub = (x["u0"] - wq[d, p][:PAIR]).astype(BF16)
                o = wq[d, p][PAIR:] + jnp.dot(x["qk"], ub, preferred_element_type=F32)
                st[d, p] = x["gl"] * st[d, p] + lax.dot_general(x["kd"], ub, tr_lhs, preferred_element_type=F32)
                for e in range(2):
                    hh = 2 * p + e
                    refs[d][5][x["sl"], hh * DK:(hh + 1) * DK] = o[e * CHUNK:(e + 1) * CHUNK]
            yield

    def run(gen, filler):
        while True:
            try:
                next(gen)
            except StopIteration as done:
                return done.value
            if filler is not None:
                next(filler, None)

    groups = [list(range(g, min(g + LOCKSTEP, nch))) for g in range(0, nch, LOCKSTEP)]
    filler = None
    for steps in groups:
        loc = run(local(steps), filler)
        if filler is not None:
            for _ in filler:
                pass
        filler = scan(loc, steps)
    for _ in filler:
        pass
    for d, p in dps:
        s_ref[d, 2 * p:2 * p + 2] = st[d, p].reshape(2, DK, DK)


def _delta_call(q, k, v, bd, bdt, s0, l, seq):
    ntok = bd.shape[0]
    td = min(TD, seq)
    nb = seq // td
    bsz = ntok // seq
    nch = td // CHUNK
    fw = lambda b, j: b * nb + j
    bw = lambda b, j: b * nb + nb - 1 - j

    def specs(t):
        hs = pl.BlockSpec((H_A, td, DK), lambda b, j: (0, t(b, j), 0))
        return [hs, hs, hs, pl.BlockSpec((td, 128), lambda b, j: (t(b, j), 0)),
                pl.BlockSpec((nch, H_A, PAIR), lambda b, j: (t(b, j), 0, 0))]

    return pl.pallas_call(
        _delta_kernel,
        out_shape=[jax.ShapeDtypeStruct((ntok, W_BR), F32), jax.ShapeDtypeStruct((ntok, W_BR), F32),
                   jax.ShapeDtypeStruct((bsz, 2, H_A, DK, DK), F32)],
        grid=(bsz, nb),
        in_specs=specs(fw) + specs(bw) + [pl.BlockSpec((None, None, 2, H_A, DK, DK),
                                                       lambda b, j: (b, l, 0, 0, 0, 0))],
        out_specs=[pl.BlockSpec((td, W_BR), lambda b, j: (fw(b, j), 0)),
                   pl.BlockSpec((td, W_BR), lambda b, j: (bw(b, j), 0)),
                   pl.BlockSpec((None, 2, H_A, DK, DK), lambda b, j: (b, 0, 0, 0, 0))],
        compiler_params=_cparams(("arbitrary", "arbitrary")), name="delta_scan",
    )(q, k, v, bd, bdt, q, k, v, bd, bdt, s0)


def _fourier_kernel(cr_ref, sr_ref, re_ref, im_ref, szc_ref, o_ref):
    y = (jnp.dot(cr_ref[...], re_ref[...], preferred_element_type=F32)
         + jnp.dot(sr_ref[...], im_ref[...], preferred_element_type=F32))
    o_ref[...] = (y * szc_ref[...]).astype(BF16)


def _fourier_call(cr, sr, re, im, szc, tn):
    bsz, r, n = re.shape
    blk = pl.BlockSpec((None, r, tn), lambda b, c: (b, 0, c))
    mat = pl.BlockSpec((r, r), lambda b, c: (0, 0))
    return pl.pallas_call(
        _fourier_kernel, out_shape=jax.ShapeDtypeStruct((bsz, r, n), BF16),
        grid=(bsz, n // tn), in_specs=[mat, mat, blk, blk, blk], out_specs=blk,
        compiler_params=_cparams(("arbitrary", "arbitrary")), name="fourier_rows",
    )(cr, sr, re, im, szc)


def _fourier_grid_kernel(cr_ref, sr_ref, re_ref, im_ref, szc_ref, o_ref):
    r, s, w = re_ref.shape
    y = (jnp.dot(cr_ref[...], re_ref[...].reshape(r * s, w), preferred_element_type=F32)
         + jnp.dot(sr_ref[...], im_ref[...].reshape(r * s, w), preferred_element_type=F32))
    o_ref[...] = (y * szc_ref[...].reshape(r * s, w)).astype(BF16).reshape(r, s, w)


def _fourier_grid_call(crk, srk, re, im, szc):
    bsz, r, gw, w = re.shape
    blk = pl.BlockSpec((None, r, COLS_SLAB, w), lambda b, c: (b, 0, c, 0))
    mat = pl.BlockSpec(crk.shape, lambda b, c: (0, 0))
    return pl.pallas_call(
        _fourier_grid_kernel, out_shape=jax.ShapeDtypeStruct((bsz, r, gw, w), BF16),
        grid=(bsz, gw // COLS_SLAB), in_specs=[mat, mat, blk, blk, blk], out_specs=blk,
        compiler_params=_cparams(("arbitrary", "arbitrary")), name="fourier_grid_rows",
    )(crk, srk, re, im, szc)


def _merge_kernel(h_ref, of_ref, ob_ref, sza_ref, osgu_ref, oc_ref, sg_ref, gate_ref, ong_ref, wbr_ref, wout_ref,
                  fing_ref, o_ref, *, final):
    o = of_ref[...] + ob_ref[...]
    sza = sza_ref[...]
    ong = ong_ref[...]
    parts = []
    for hh in range(H_A):
        oh = o[:, hh * DK:(hh + 1) * DK]
        oh = (oh * lax.rsqrt(jnp.mean(oh * oh, axis=-1, keepdims=True) + EPS)) * ong
        parts.append((oh * sza[:, hh * DK:(hh + 1) * DK]).astype(BF16))
    o_a = jnp.concatenate(parts, axis=1)
    merged = (sg_ref[0] * jnp.dot(o_a, wbr_ref[0], preferred_element_type=F32)
              + sg_ref[1] * jnp.dot(osgu_ref[...], wbr_ref[1], preferred_element_type=F32)
              + sg_ref[2] * jnp.dot(oc_ref[...], wbr_ref[2], preferred_element_type=F32))
    out = jnp.dot(merged.astype(BF16), wout_ref[...], preferred_element_type=F32)
    hn = h_ref[...] + gate_ref[...] * out
    if final:
        hn = (hn * lax.rsqrt(jnp.mean(hn * hn, axis=-1, keepdims=True) + EPS)) * fing_ref[...]
    o_ref[...] = hn


def _merge_call(h, o_f, o_b, sza, osgu, oc, sg, gate, row_of_tile, l, wts, final_g, final, tm):
    ntok, d = h.shape
    tile = lambda w: pl.BlockSpec((tm, w), lambda i: (i, 0))
    lws = [wts[n] for n in ("o_norm_g", "w_branch", "w_out")]
    in_specs = [tile(d), tile(W_BR), tile(W_BR), tile(W_BR), tile(W_BR), tile(W_BR),
                pl.BlockSpec((3, tm, d), lambda i: (0, i, 0)),
                pl.BlockSpec((None, 1, d), lambda i: (row_of_tile(i * (tm // TM)), 0, 0))
                ] + [_layer_spec(a, l) for a in lws] + [_full_spec(final_g)]
    return pl.pallas_call(
        functools.partial(_merge_kernel, final=final),
        out_shape=jax.ShapeDtypeStruct((ntok, d), F32), grid=(ntok // tm,),
        in_specs=in_specs, out_specs=tile(d),
        compiler_params=_cparams(("arbitrary",)), name="merge_out",
    )(h, o_f, o_b, sza, osgu, oc, sg, gate, *lws, final_g)


def _dft(n):
    idx = np.arange(n)
    ang = 2.0 * np.pi * ((idx[:, None] * idx[None, :]) % n) / n
    return np.cos(ang) / np.sqrt(n), np.sin(ang) / np.sqrt(n)


def _consts(seq, rows):
    cc, sc = _dft(CH_C)
    cg, sg = _dft(GRID_W)
    cp, sp = _dft(seq)
    cr, sr = _dft(rows)
    blk = np.kron(np.eye(TM // CHUNK), np.tril(np.ones((CHUNK, CHUNK))))
    bf = lambda a: jnp.asarray(a, F32).astype(BF16)
    return {
        "cch": bf(np.concatenate([cc, sc], axis=1)),
        "mc": bf(np.block([[cg, -sg], [-sg, -cg]])),
        "ltri": bf(blk), "utri": bf(blk.T),
        "cp": bf(cp), "sp": bf(sp),
        "crk": bf(np.kron(cr, np.eye(COLS_SLAB))), "srk": bf(np.kron(sr, np.eye(COLS_SLAB))),
    }


def _stacked_weights(w_in, ln_g, conv_w, a_log, dt_bias, o_norm_g, sgu_norm_g, w_spatial, b_spatial, w_branch, w_out):
    depth, d, _ = w_in.shape
    nq = 3 * W_BR
    nbd = 4 * H_A
    adt = jnp.stack([a_log.reshape(depth, -1), dt_bias.reshape(depth, -1)], axis=1)
    return {
        "ln_g": ln_g.reshape(depth, 1, d),
        "w_in": jnp.concatenate([w_in[:, :, :nq], jnp.pad(w_in[:, :, nq:nq + nbd], ((0, 0), (0, 0), (0, 128 - nbd))),
                                 w_in[:, :, nq + nbd:]], axis=-1).astype(BF16),
        "conv_w": conv_w,
        "adt": jnp.pad(adt, ((0, 0), (0, 0), (2 * H_A, 128 - nbd))),
        "sgu_g": sgu_norm_g.reshape(depth, 1, W_BR),
        "w_sp": w_spatial.astype(BF16),
        "b_sp": jnp.broadcast_to(b_spatial[..., None], b_spatial.shape + (W_BR // G_B,)),
        "o_norm_g": o_norm_g.reshape(depth, 1, DK),
        "w_branch": w_branch.astype(BF16),
        "w_out": w_out.astype(BF16),
    }


def _group_forward(x, mods, row_of_tile, s0_all, grid_cols, wts, consts, final_g):
    bsz, t, d = x.shape
    ntok = bsz * t
    nb = t // TM
    h = x.reshape(ntok, d)
    tm_merge = TMM if (t % TMM == 0 or not grid_cols) and ntok % TMM == 0 else TM
    states = []
    depth = mods.shape[0]
    for l in range(depth):
        shift, scale, gate = mods[l, 0], mods[l, 1], mods[l, 2]
        q, k, v, bd, sza, osgu, re, im, szc, sg = _in_call(h, shift, scale, row_of_tile, nb, grid_cols, l, wts, consts)
        bdt = bd[:, 2 * H_A:4 * H_A].reshape(ntok // CHUNK, CHUNK, H_A, 2).transpose(0, 2, 3, 1).reshape(
            ntok // CHUNK, H_A, PAIR)
        o_f, o_b, s_fin = _delta_call(q, k, v, bd, bdt, s0_all, min(l, s0_all.shape[1] - 1), t)
        states.append(s_fin)
        if grid_cols:
            shp = (bsz, t // GRID_W, GRID_W, W_BR)
            oc = _fourier_grid_call(consts["crk"], consts["srk"], re.reshape(shp), im.reshape(shp), szc.reshape(shp))
        else:
            shp = (bsz, t, W_BR)
            oc = _fourier_call(consts["cp"], consts["sp"], re.reshape(shp), im.reshape(shp), szc.reshape(shp), W_BR)
        h = _merge_call(h, o_f, o_b, sza, osgu, oc.reshape(ntok, W_BR), sg, gate, row_of_tile, l, wts,
                        final_g.reshape(1, d), l == depth - 1, tm_merge)
    return h.reshape(bsz, t, d), states


def kernel(x_prompt, x_sample, state_delta, c, c_ctx, ln_g, w_ada, b_ada, w_in, conv_w, a_log, dt_bias, o_norm_g,
           sgu_norm_g, w_spatial, b_spatial, w_branch, w_out, final_g):
    depth, d = ln_g.shape
    bp, tp, _ = x_prompt.shape
    bs, ts, _ = x_sample.shape
    consts = _consts(tp, ts // GRID_W)
    wts = _stacked_weights(w_in, ln_g, conv_w, a_log, dt_bias, o_norm_g, sgu_norm_g, w_spatial, b_spatial,
                           w_branch, w_out)
    nrow = -(-(1 + bs) // 8) * 8
    cvecs = jnp.concatenate([c_ctx[None], c, jnp.zeros((nrow - 1 - bs, d), F32)], axis=0)
    mods = _ada_call(cvecs, w_ada, b_ada).reshape(depth, nrow, 3, 1, d).transpose(0, 2, 1, 3, 4)

    zeros = jnp.zeros((bp, 1, 2, H_A, DK, DK), F32)
    y_prompt, st = _group_forward(x_prompt, mods, lambda i: 0, zeros, False, wts, consts, final_g)
    tiles_s = ts // TM
    y_sample, _ = _group_forward(x_sample, mods, lambda i: 1 + i // tiles_s, state_delta, True, wts, consts, final_g)
    return y_prompt, y_sample, jnp.stack(st, axis=1)
```

```python
import functools

import numpy as np
import jax
import jax.numpy as jnp
from jax import lax
from jax.experimental import pallas as pl
from jax.experimental.pallas import tpu as pltpu

F32 = jnp.float32
BF16 = jnp.bfloat16

W_BR = 512
H_A = 4
DK = 128
CONV_K = 3
CHUNK = 64
PAIR = 2 * CHUNK
G_B = 4
CHUNK_B = 128
G_C = 4
CH_C = W_BR // G_C
GRID_W = 64
COLS_SLAB = 16
EPS = 1e-6
C_BD = 3 * W_BR
C_REST = C_BD + 128
TM = 256
TMM = 512
TD = 1024
LOCKSTEP = 2
HALO = 16
VMEM_LIMIT = 56 * 1024 * 1024


def _bdot(a, b):
    return jnp.dot(a.astype(BF16), b.astype(BF16), preferred_element_type=F32)


def _split(x):
    hi = x.astype(BF16)
    lo = (x - hi.astype(F32)).astype(BF16)
    return hi, lo


def _sigmoid(x):
    return 0.5 * jnp.tanh(0.5 * x) + 0.5


def _silu(x):
    return x * _sigmoid(x)


def _cparams(sem):
    return pltpu.CompilerParams(dimension_semantics=sem, vmem_limit_bytes=VMEM_LIMIT)


def _layer_spec(a, l):
    return pl.BlockSpec((None,) + a.shape[1:], lambda *_: (l,) + (0,) * (a.ndim - 1))


def _full_spec(a):
    return pl.BlockSpec(a.shape, lambda *_: (0,) * a.ndim)


def _ada_kernel(c_ref, w_ref, b_ref, o_ref):
    o_ref[...] = _bdot(_silu(c_ref[...]), w_ref[...]) + b_ref[...]


def _ada_call(cvecs, w_ada, b_ada):
    depth, d, e = w_ada.shape
    r = cvecs.shape[0]
    tn = 1024
    return pl.pallas_call(
        _ada_kernel,
        out_shape=jax.ShapeDtypeStruct((depth, r, e), F32),
        grid=(depth, e // tn),
        in_specs=[pl.BlockSpec((r, d), lambda l, n: (0, 0)),
                  pl.BlockSpec((None, d, tn), lambda l, n: (l, 0, n)),
                  pl.BlockSpec((None, 1, tn), lambda l, n: (l, 0, n))],
        out_specs=pl.BlockSpec((None, r, tn), lambda l, n: (l, 0, n)),
        compiler_params=_cparams(("arbitrary", "arbitrary")),
        name="ada_mod",
    )(cvecs, w_ada, b_ada.reshape(depth, 1, e))


def _regroup_kernel(w_ref, o_ref):
    nq, nbd = C_BD, 4 * H_A
    o_ref[:, :nq] = w_ref[:, :nq].astype(BF16)
    o_ref[:, nq:C_REST] = jnp.concatenate(
        [w_ref[:, nq:nq + nbd], jnp.zeros((w_ref.shape[0], C_REST - nq - nbd), F32)], axis=1).astype(BF16)
    o_ref[:, C_REST:] = w_ref[:, nq + nbd:].astype(BF16)


def _regroup_call(w_in):
    depth, d, n = w_in.shape
    n_out = n + C_REST - C_BD - 4 * H_A
    rows = 256
    return pl.pallas_call(
        _regroup_kernel, out_shape=jax.ShapeDtypeStruct((depth, d, n_out), BF16), grid=(depth, d // rows),
        in_specs=[pl.BlockSpec((None, rows, n), lambda l, r: (l, r, 0))],
        out_specs=pl.BlockSpec((None, rows, n_out), lambda l, r: (l, r, 0)),
        compiler_params=_cparams(("arbitrary", "arbitrary")), name="regroup_w_in",
    )(w_in)


def _in_kernel(x_ref, xp_ref, xn_ref, shift_ref, scale_ref, lng_ref, win_ref,
               convw_ref, adt_ref, sgug_ref, wsp_ref, bsp_ref, cch_ref, mc_ref, ltri_ref, utri_ref,
               q_ref, k_ref, v_ref, bd_ref, bdt_ref, sza_ref, osgu_ref, re_ref, im_ref, szc_ref, sg_ref,
               *, seq_tiles, grid_cols):
    i = pl.program_id(0)
    t_in_seq = i % seq_tiles
    lng = lng_ref[...]
    scale1 = 1.0 + scale_ref[...]
    shift = shift_ref[...]

    x = jnp.concatenate([xp_ref[...], x_ref[...], xn_ref[...]], axis=0)
    y = x * lax.rsqrt(jnp.mean(x * x, axis=-1, keepdims=True) + EPS)
    xe = ((y * lng) * scale1 + shift).astype(BF16)
    xn = xe[HALO:HALO + TM]

    def rest(j):
        return jnp.dot(xn, win_ref[:, C_REST + j * W_BR:C_REST + (j + 1) * W_BR], preferred_element_type=F32)

    pe = jnp.dot(xe, win_ref[:, :C_BD], preferred_element_type=F32)
    pbd = jnp.dot(xn, win_ref[:, C_BD:C_REST], preferred_element_type=F32)
    z_a = rest(0)

    p_prev = jnp.where(t_in_seq == 0, 0.0, pe[HALO - 1:HALO])
    p_next = jnp.where(t_in_seq == seq_tiles - 1, 0.0, pe[HALO + TM:HALO + TM + 1])
    row = lax.broadcasted_iota(jnp.int32, (TM, 1), 0)
    cw = convw_ref[...]

    def conv_silu(c0):
        cs = slice(c0, c0 + DK)
        p = pe[HALO:HALO + TM, cs]
        down = jnp.where(row == 0, p_prev[:, cs], pltpu.roll(p, 1, axis=0))
        up = jnp.where(row == TM - 1, p_next[:, cs], pltpu.roll(p, TM - 1, axis=0))
        return _silu(cw[0:1, cs] * down + cw[1:2, cs] * p + cw[2:3, cs] * up)

    def qkv_head(h):
        qh = conv_silu(h * DK)
        kh = conv_silu(W_BR + h * DK)
        q_ref[h] = qh * lax.rsqrt(jnp.sum(qh * qh, axis=-1, keepdims=True) + EPS) * (DK ** -0.5)
        k_ref[h] = kh * lax.rsqrt(jnp.sum(kh * kh, axis=-1, keepdims=True) + EPS)
        v_ref[h] = conv_silu(2 * W_BR + h * DK)

    def gate_piece(m):
        return jnp.dot(xn, win_ref[:, C_REST + (6 + m) * W_BR:C_REST + (7 + m) * W_BR], preferred_element_type=F32)

    def store_gate(m, gp):
        sg_ref[m // 2, :, (m % 2) * W_BR:(m % 2 + 1) * W_BR] = _sigmoid(gp).astype(BF16)

    u_b = rest(1)
    qkv_head(0)
    v_b = rest(2)
    qkv_head(1)
    z_b = rest(3)
    qkv_head(2)
    x_c = rest(4).astype(BF16)
    qkv_head(3)
    z_c = rest(5)

    adt = adt_ref[...]
    sp_in = pbd + adt[1:2]
    softplus = jnp.maximum(sp_in, 0.0) + jnp.log1p(jnp.exp(-jnp.abs(sp_in)))
    g = -jnp.exp(adt[0:1]) * softplus
    lane = lax.broadcasted_iota(jnp.int32, (TM, 128), 1)
    g = jnp.where((lane >= 2 * H_A) & (lane < 4 * H_A), g, 0.0)
    g1 = g.astype(BF16).astype(F32)
    g2 = (g - g1).astype(BF16).astype(F32)
    g3 = (g - g1 - g2).astype(BF16).astype(F32)
    packed = (g1 + pltpu.roll(g2, 2 * H_A, axis=1) + pltpu.roll(g3, 4 * H_A, axis=1)).astype(BF16)

    def unpack(r):
        return r + pltpu.roll(r, 128 - 2 * H_A, axis=1) + pltpu.roll(r, 128 - 4 * H_A, axis=1)

    gpre = unpack(jnp.dot(ltri_ref[...], packed, preferred_element_type=F32))
    gsuf = unpack(jnp.dot(utri_ref[...], packed, preferred_element_type=F32))
    gam = jnp.where(lane < 3 * H_A, gpre, gsuf)
    bd_ref[...] = jnp.where(lane < 2 * H_A, _sigmoid(pbd), gam)
    gam_t = jnp.transpose(gam)
    for c in range(TM // CHUNK):
        for dp in range(H_A):
            for e in range(2):
                r = 2 * H_A + 2 * dp + e
                bdt_ref[c, dp:dp + 1, e * CHUNK:(e + 1) * CHUNK] = gam_t[r:r + 1, c * CHUNK:(c + 1) * CHUNK]
    sza_ref[...] = _silu(z_a).astype(BF16)

    vn = (v_b * lax.rsqrt(jnp.mean(v_b * v_b, axis=-1, keepdims=True) + EPS) * sgug_ref[...]).astype(BF16)
    gate_b = _silu(z_b)

    def sgu_chunk(n):
        rs = slice(n * CHUNK_B, (n + 1) * CHUNK_B)
        for gi in range(G_B):
            cs = slice(gi * (W_BR // G_B), (gi + 1) * (W_BR // G_B))
            vs = jnp.dot(wsp_ref[gi], vn[rs, cs], preferred_element_type=F32) + bsp_ref[gi]
            osgu_ref[rs, cs] = ((u_b[rs, cs] * vs) * gate_b[rs, cs]).astype(BF16)

    gps = [gate_piece(0)]
    for n in range(TM // CHUNK_B):
        sgu_chunk(n)
        gps.append(gate_piece(len(gps)))

    szc_ref[...] = _silu(z_c).astype(BF16)
    cch = cch_ref[...]
    a_parts, b_parts = [], []
    for gi in range(G_C):
        ab = jnp.dot(x_c[:, gi * CH_C:(gi + 1) * CH_C], cch, preferred_element_type=F32)
        a_parts.append(ab[:, :CH_C])
        b_parts.append(ab[:, CH_C:])
    a = jnp.concatenate(a_parts, axis=1)
    b = jnp.concatenate(b_parts, axis=1)
    gps.append(gate_piece(len(gps)))
    if grid_cols:
        mc = mc_ref[...]
        for r in range(TM // GRID_W):
            rs = slice(r * GRID_W, (r + 1) * GRID_W)
            stacked = jnp.concatenate([a[rs], b[rs]], axis=0).astype(BF16)
            z = jnp.dot(mc, stacked, preferred_element_type=F32)
            re_ref[rs, :] = z[:GRID_W].astype(BF16)
            im_ref[rs, :] = z[GRID_W:].astype(BF16)
    else:
        re_ref[...] = a.astype(BF16)
        im_ref[...] = (-b).astype(BF16)

    for m in range(6):
        if len(gps) < 6:
            gps.append(gate_piece(len(gps)))
        store_gate(m, gps[m])


def _in_call(h, shift, scale, row_of_tile, seq_tiles, grid_cols, l, wts, consts):
    ntok, d = h.shape
    nt = ntok // TM
    nhb = ntok // HALO
    tile = lambda w: pl.BlockSpec((TM, w), lambda i: (i, 0))
    mod = pl.BlockSpec((None, 1, d), lambda i: (row_of_tile(i), 0, 0))
    lws = [wts[n] for n in ("ln_g", "w_in", "conv_w", "adt", "sgu_g", "w_sp", "b_sp")]
    cs = [consts[n] for n in ("cch", "mc", "ltri", "utri")]
    in_specs = [tile(d),
                pl.BlockSpec((HALO, d), lambda i: (jnp.maximum(i * (TM // HALO) - 1, 0), 0)),
                pl.BlockSpec((HALO, d), lambda i: (jnp.minimum((i + 1) * (TM // HALO), nhb - 1), 0)),
                mod, mod] + [_layer_spec(a, l) for a in lws] + [_full_spec(a) for a in cs]
    hsplit = pl.BlockSpec((H_A, TM, DK), lambda i: (0, i, 0))
    out_shape = [jax.ShapeDtypeStruct((H_A, ntok, DK), F32)] * 3 + [
        jax.ShapeDtypeStruct((ntok, 128), F32),
        jax.ShapeDtypeStruct((ntok // CHUNK, H_A, PAIR), F32),
        jax.ShapeDtypeStruct((ntok, W_BR), BF16),
        jax.ShapeDtypeStruct((ntok, W_BR), BF16),
        jax.ShapeDtypeStruct((ntok, W_BR), BF16),
        jax.ShapeDtypeStruct((ntok, W_BR), BF16),
        jax.ShapeDtypeStruct((ntok, W_BR), BF16),
        jax.ShapeDtypeStruct((3, ntok, 2 * W_BR), BF16),
    ]
    out_specs = [hsplit] * 3 + [tile(128), pl.BlockSpec((TM // CHUNK, H_A, PAIR), lambda i: (i, 0, 0)), tile(W_BR), tile(W_BR), tile(W_BR), tile(W_BR), tile(W_BR),
                                pl.BlockSpec((3, TM, 2 * W_BR), lambda i: (0, i, 0))]
    return pl.pallas_call(
        functools.partial(_in_kernel, seq_tiles=seq_tiles, grid_cols=grid_cols),
        out_shape=out_shape, grid=(nt,), in_specs=in_specs, out_specs=out_specs,
        compiler_params=_cparams(("arbitrary",)), name="in_proj",
    )(h, h, h, shift, scale, *lws, *cs)


def _dot3(x, y):
    xh, xl = _split(x)
    yh, yl = _split(y)
    n = y.shape[1]
    rhs = jnp.concatenate([jnp.concatenate([yh, yl], axis=1),
                           jnp.concatenate([yh, jnp.zeros_like(yh)], axis=1)], axis=0)
    z = jnp.dot(jnp.concatenate([xh, xl], axis=1), rhs, preferred_element_type=F32)
    return z[:, :n] + z[:, n:]


def _delta_kernel(qf_ref, kf_ref, vf_ref, bdf_ref, bdtf_ref, qb_ref, kb_ref, vb_ref, bdb_ref, bdtb_ref, s0_ref,
                  of_ref, ob_ref, s_ref):
    j = pl.program_id(1)

    @pl.when(j == 0)
    def _():
        s_ref[...] = s0_ref[...]

    nch = qf_ref.shape[1] // CHUNK
    ri = lax.broadcasted_iota(jnp.int32, (PAIR, PAIR), 0)
    ci = lax.broadcasted_iota(jnp.int32, (PAIR, PAIR), 1)
    same = (ri < CHUNK) == (ci < CHUNK)
    eye = (ri == ci).astype(F32)
    masks = (same & (ri > ci), same & (ri < ci))
    top = lax.broadcasted_iota(jnp.int32, (PAIR, 1), 0) < CHUNK
    refs = ((qf_ref, kf_ref, vf_ref, bdf_ref, bdtf_ref, of_ref), (qb_ref, kb_ref, vb_ref, bdb_ref, bdtb_ref, ob_ref))
    tr_rhs = (((1,), (1,)), ((), ()))
    tr_lhs = (((0,), (0,)), ((), ()))
    dps = [(d, p) for d in range(2) for p in range(H_A // 2)]

    def blockdiag(x):
        return jnp.concatenate([jnp.where(top, x, 0.0), jnp.where(top, 0.0, x)], axis=1)

    def local(steps):
        probs = [(s, d, p) for s in steps for d, p in dps]
        n = len(probs)
        ld = []
        for s, d, p in probs:
            q_ref, k_ref, v_ref, bd_ref, bdt_ref, _ = refs[d]
            c = s if d == 0 else nch - 1 - s
            sl = slice(c * CHUNK, (c + 1) * CHUNK)
            pair = lambda r: jnp.concatenate([r[2 * p, sl, :], r[2 * p + 1, sl, :]], axis=0)
            col = d * H_A + 2 * p
            colpair = lambda c0: jnp.concatenate([bd_ref[sl, c0:c0 + 1], bd_ref[sl, c0 + 1:c0 + 2]], axis=0)
            gam = colpair(2 * H_A + col)
            last = CHUNK - 1 if d == 0 else 0
            tots = [gam[e * CHUNK + last:e * CHUNK + last + 1] for e in range(2)]
            ld.append(dict(q=pair(q_ref), k=pair(k_ref), v=pair(v_ref), beta=colpair(col), gam=gam,
                           gam_row=bdt_ref[c, 2 * d + p:2 * d + p + 1, :], tots=tots, sl=sl))
        yield
        qkk, dec = [], []
        for i, (s, d, p) in enumerate(probs):
            x = ld[i]
            kb = x["k"].astype(BF16)
            qkk.append(lax.dot_general(jnp.concatenate([x["q"].astype(BF16), kb], axis=0), kb, tr_rhs,
                                       preferred_element_type=F32))
            m = masks[d]
            dec.append(jnp.where(m, jnp.exp(jnp.where(m, x["gam"] - x["gam_row"], 0.0)), 0.0))
        yield
        pw = [-((ld[i]["beta"] * qkk[i][PAIR:]) * dec[i]) for i in range(n)]
        t = [eye + pw[i] for i in range(n)]
        pw = [_dot3(pw[i], pw[i]) for i in range(n)]
        yield
        for _ in range(4):
            tp = [_dot3(jnp.concatenate([t[i], pw[i]], axis=0), pw[i]) for i in range(n)]
            t = [t[i] + tp[i][:PAIR] for i in range(n)]
            pw = [tp[i][PAIR:] for i in range(n)]
            yield
        t = [t[i] + _dot3(t[i], pw[i]) for i in range(n)]
        yield
        eg = [jnp.exp(ld[i]["gam"]) for i in range(n)]
        sol = [_dot3(t[i], jnp.concatenate([ld[i]["v"] * ld[i]["beta"], ld[i]["k"] * (ld[i]["beta"] * eg[i])], axis=1))
               for i in range(n)]
        yield
        out = {}
        for i, (s, d, p) in enumerate(probs):
            x = ld[i]
            tot = jnp.concatenate([jnp.broadcast_to(tt, (CHUNK, 1)) for tt in x["tots"]], axis=0)
            out[s, d, p] = dict(
                lhs=jnp.concatenate([blockdiag(sol[i][:, DK:]), blockdiag(x["q"] * eg[i])], axis=0).astype(BF16),
                u0=sol[i][:, :DK],
                qk=(qkk[i][:PAIR] * (dec[i] + eye)).astype(BF16),
                kd=blockdiag(x["k"] * jnp.exp(tot - x["gam"])).astype(BF16),
                gl=jnp.concatenate([jnp.broadcast_to(jnp.exp(tt), (DK, 1)) for tt in x["tots"]], axis=0),
                sl=x["sl"])
        return out

    st = {dp: s_ref[dp[0], 2 * dp[1]:2 * dp[1] + 2].reshape(2 * DK, DK) for dp in dps}

    def scan(loc, steps):
        for s in steps:
            wq = {dp: jnp.dot(loc[(s,) + dp]["lhs"], st[dp].astype(BF16), preferred_element_type=F32) for dp in dps}
            yield
            for d, p in dps:
                x = loc[s, d, p]
                ub = (x["u0"] - wq[d, p][:PAIR]).astype(BF16)
                o = wq[d, p][PAIR:] + jnp.dot(x["qk"], ub, preferred_element_type=F32)
                st[d, p] = x["gl"] * st[d, p] + lax.dot_general(x["kd"], ub, tr_lhs, preferred_element_type=F32)
                for e in range(2):
                    hh = 2 * p + e
                    refs[d][5][x["sl"], hh * DK:(hh + 1) * DK] = o[e * CHUNK:(e + 1) * CHUNK]
            yield

    def run(gen, filler):
        while True:
            try:
                next(gen)
            except StopIteration as done:
                return done.value
            if filler is not None:
                next(filler, None)

    groups = [list(range(g, min(g + LOCKSTEP, nch))) for g in range(0, nch, LOCKSTEP)]
    filler = None
    for steps in groups:
        loc = run(local(steps), filler)
        if filler is not None:
            for _ in filler:
                pass
        filler = scan(loc, steps)
    for _ in filler:
        pass
    for d, p in dps:
        s_ref[d, 2 * p:2 * p + 2] = st[d, p].reshape(2, DK, DK)


def _delta_call(q, k, v, bd, bdt, s0, l, seq):
    ntok = bd.shape[0]
    td = min(TD, seq)
    nb = seq // td
    bsz = ntok // seq
    nch = td // CHUNK
    fw = lambda b, j: b * nb + j
    bw = lambda b, j: b * nb + nb - 1 - j

    def specs(t):
        hs = pl.BlockSpec((H_A, td, DK), lambda b, j: (0, t(b, j), 0))
        return [hs, hs, hs, pl.BlockSpec((td, 128), lambda b, j: (t(b, j), 0)),
                pl.BlockSpec((nch, H_A, PAIR), lambda b, j: (t(b, j), 0, 0))]

    return pl.pallas_call(
        _delta_kernel,
        out_shape=[jax.ShapeDtypeStruct((ntok, W_BR), F32), jax.ShapeDtypeStruct((ntok, W_BR), F32),
                   jax.ShapeDtypeStruct((bsz, 2, H_A, DK, DK), F32)],
        grid=(bsz, nb),
        in_specs=specs(fw) + specs(bw) + [pl.BlockSpec((None, None, 2, H_A, DK, DK),
                                                       lambda b, j: (b, l, 0, 0, 0, 0))],
        out_specs=[pl.BlockSpec((td, W_BR), lambda b, j: (fw(b, j), 0)),
                   pl.BlockSpec((td, W_BR), lambda b, j: (bw(b, j), 0)),
                   pl.BlockSpec((None, 2, H_A, DK, DK), lambda b, j: (b, 0, 0, 0, 0))],
        compiler_params=_cparams(("arbitrary", "arbitrary")), name="delta_scan",
    )(q, k, v, bd, bdt, q, k, v, bd, bdt, s0)


def _fourier_kernel(cr_ref, sr_ref, re_ref, im_ref, szc_ref, o_ref):
    y = (jnp.dot(cr_ref[...], re_ref[...], preferred_element_type=F32)
         + jnp.dot(sr_ref[...], im_ref[...], preferred_element_type=F32))
    o_ref[...] = (y * szc_ref[...]).astype(BF16)


def _fourier_call(cr, sr, re, im, szc, tn):
    bsz, r, n = re.shape
    blk = pl.BlockSpec((None, r, tn), lambda b, c: (b, 0, c))
    mat = pl.BlockSpec((r, r), lambda b, c: (0, 0))
    return pl.pallas_call(
        _fourier_kernel, out_shape=jax.ShapeDtypeStruct((bsz, r, n), BF16),
        grid=(bsz, n // tn), in_specs=[mat, mat, blk, blk, blk], out_specs=blk,
        compiler_params=_cparams(("arbitrary", "arbitrary")), name="fourier_rows",
    )(cr, sr, re, im, szc)


def _fourier_grid_kernel(cr_ref, sr_ref, re_ref, im_ref, szc_ref, o_ref):
    r, s, w = re_ref.shape
    y = (jnp.dot(cr_ref[...], re_ref[...].reshape(r * s, w), preferred_element_type=F32)
         + jnp.dot(sr_ref[...], im_ref[...].reshape(r * s, w), preferred_element_type=F32))
    o_ref[...] = (y * szc_ref[...].reshape(r * s, w)).astype(BF16).reshape(r, s, w)


def _fourier_grid_call(crk, srk, re, im, szc):
    bsz, r, gw, w = re.shape
    blk = pl.BlockSpec((None, r, COLS_SLAB, w), lambda b, c: (b, 0, c, 0))
    mat = pl.BlockSpec(crk.shape, lambda b, c: (0, 0))
    return pl.pallas_call(
        _fourier_grid_kernel, out_shape=jax.ShapeDtypeStruct((bsz, r, gw, w), BF16),
        grid=(bsz, gw // COLS_SLAB), in_specs=[mat, mat, blk, blk, blk], out_specs=blk,
        compiler_params=_cparams(("arbitrary", "arbitrary")), name="fourier_grid_rows",
    )(crk, srk, re, im, szc)


def _merge_kernel(h_ref, of_ref, ob_ref, sza_ref, osgu_ref, oc_ref, sg_ref, gate_ref, ong_ref, wbr_ref, wout_ref,
                  fing_ref, o_ref, *, final):
    o = of_ref[...] + ob_ref[...]
    sza = sza_ref[...]
    ong = ong_ref[...]
    parts = []
    for hh in range(H_A):
        oh = o[:, hh * DK:(hh + 1) * DK]
        oh = (oh * lax.rsqrt(jnp.mean(oh * oh, axis=-1, keepdims=True) + EPS)) * ong
        parts.append((oh * sza[:, hh * DK:(hh + 1) * DK]).astype(BF16))
    o_a = jnp.concatenate(parts, axis=1)
    merged = (sg_ref[0] * jnp.dot(o_a, wbr_ref[0], preferred_element_type=F32)
              + sg_ref[1] * jnp.dot(osgu_ref[...], wbr_ref[1], preferred_element_type=F32)
              + sg_ref[2] * jnp.dot(oc_ref[...], wbr_ref[2], preferred_element_type=F32))
    out = jnp.dot(merged.astype(BF16), wout_ref[...], preferred_element_type=F32)
    hn = h_ref[...] + gate_ref[...] * out
    if final:
        hn = (hn * lax.rsqrt(jnp.mean(hn * hn, axis=-1, keepdims=True) + EPS)) * fing_ref[...]
    o_ref[...] = hn


def _merge_call(h, o_f, o_b, sza, osgu, oc, sg, gate, row_of_tile, l, wts, final_g, final, tm):
    ntok, d = h.shape
    tile = lambda w: pl.BlockSpec((tm, w), lambda i: (i, 0))
    lws = [wts[n] for n in ("o_norm_g", "w_branch", "w_out")]
    in_specs = [tile(d), tile(W_BR), tile(W_BR), tile(W_BR), tile(W_BR), tile(W_BR),
                pl.BlockSpec((3, tm, d), lambda i: (0, i, 0)),
                pl.BlockSpec((None, 1, d), lambda i: (row_of_tile(i * (tm // TM)), 0, 0))
                ] + [_layer_spec(a, l) for a in lws] + [_full_spec(final_g)]
    return pl.pallas_call(
        functools.partial(_merge_kernel, final=final),
        out_shape=jax.ShapeDtypeStruct((ntok, d), F32), grid=(ntok // tm,),
        in_specs=in_specs, out_specs=tile(d),
        compiler_params=_cparams(("arbitrary",)), name="merge_out",
    )(h, o_f, o_b, sza, osgu, oc, sg, gate, *lws, final_g)


def _dft(n):
    idx = np.arange(n)
    ang = 2.0 * np.pi * ((idx[:, None] * idx[None, :]) % n) / n
    return np.cos(ang) / np.sqrt(n), np.sin(ang) / np.sqrt(n)


def _consts(seq, rows):
    cc, sc = _dft(CH_C)
    cg, sg = _dft(GRID_W)
    cp, sp = _dft(seq)
    cr, sr = _dft(rows)
    blk = np.kron(np.eye(TM // CHUNK), np.tril(np.ones((CHUNK, CHUNK))))
    bf = lambda a: jnp.asarray(a, F32).astype(BF16)
    return {
        "cch": bf(np.concatenate([cc, sc], axis=1)),
        "mc": bf(np.block([[cg, -sg], [-sg, -cg]])),
        "ltri": bf(blk), "utri": bf(blk.T),
        "cp": bf(cp), "sp": bf(sp),
        "crk": bf(np.kron(cr, np.eye(COLS_SLAB))), "srk": bf(np.kron(sr, np.eye(COLS_SLAB))),
    }


def _stacked_weights(w_in, ln_g, conv_w, a_log, dt_bias, o_norm_g, sgu_norm_g, w_spatial, b_spatial, w_branch, w_out):
    depth, d, _ = w_in.shape
    nq = 3 * W_BR
    nbd = 4 * H_A
    adt = jnp.stack([a_log.reshape(depth, -1), dt_bias.reshape(depth, -1)], axis=1)
    return {
        "ln_g": ln_g.reshape(depth, 1, d),
        "w_in": _regroup_call(w_in),
        "conv_w": conv_w,
        "adt": jnp.pad(adt, ((0, 0), (0, 0), (2 * H_A, 128 - nbd))),
        "sgu_g": sgu_norm_g.reshape(depth, 1, W_BR),
        "w_sp": w_spatial.astype(BF16),
        "b_sp": jnp.broadcast_to(b_spatial[..., None], b_spatial.shape + (W_BR // G_B,)),
        "o_norm_g": o_norm_g.reshape(depth, 1, DK),
        "w_branch": w_branch.astype(BF16),
        "w_out": w_out.astype(BF16),
    }


def _group_forward(x, mods, row_of_tile, s0_all, grid_cols, wts, consts, final_g):
    bsz, t, d = x.shape
    ntok = bsz * t
    nb = t // TM
    h = x.reshape(ntok, d)
    tm_merge = TMM if (t % TMM == 0 or not grid_cols) and ntok % TMM == 0 else TM
    states = []
    depth = mods.shape[0]
    for l in range(depth):
        shift, scale, gate = mods[l, 0], mods[l, 1], mods[l, 2]
        q, k, v, bd, bdt, sza, osgu, re, im, szc, sg = _in_call(h, shift, scale, row_of_tile, nb, grid_cols, l, wts, consts)
        o_f, o_b, s_fin = _delta_call(q, k, v, bd, bdt, s0_all, min(l, s0_all.shape[1] - 1), t)
        states.append(s_fin)
        if grid_cols:
            shp = (bsz, t // GRID_W, GRID_W, W_BR)
            oc = _fourier_grid_call(consts["crk"], consts["srk"], re.reshape(shp), im.reshape(shp), szc.reshape(shp))
        else:
            shp = (bsz, t, W_BR)
            oc = _fourier_call(consts["cp"], consts["sp"], re.reshape(shp), im.reshape(shp), szc.reshape(shp), W_BR)
        h = _merge_call(h, o_f, o_b, sza, osgu, oc.reshape(ntok, W_BR), sg, gate, row_of_tile, l, wts,
                        final_g.reshape(1, d), l == depth - 1, tm_merge)
    return h.reshape(bsz, t, d), states


def kernel(x_prompt, x_sample, state_delta, c, c_ctx, ln_g, w_ada, b_ada, w_in, conv_w, a_log, dt_bias, o_norm_g,
           sgu_norm_g, w_spatial, b_spatial, w_branch, w_out, final_g):
    depth, d = ln_g.shape
    bp, tp, _ = x_prompt.shape
    bs, ts, _ = x_sample.shape
    consts = _consts(tp, ts // GRID_W)
    wts = _stacked_weights(w_in, ln_g, conv_w, a_log, dt_bias, o_norm_g, sgu_norm_g, w_spatial, b_spatial,
                           w_branch, w_out)
    nrow = -(-(1 + bs) // 8) * 8
    cvecs = jnp.concatenate([c_ctx[None], c, jnp.zeros((nrow - 1 - bs, d), F32)], axis=0)
    mods = _ada_call(cvecs, w_ada, b_ada).reshape(depth, nrow, 3, 1, d).transpose(0, 2, 1, 3, 4)

    zeros = jnp.zeros((bp, 1, 2, H_A, DK, DK), F32)
    y_prompt, st = _group_forward(x_prompt, mods, lambda i: 0, zeros, False, wts, consts, final_g)
    tiles_s = ts // TM
    y_sample, _ = _group_forward(x_sample, mods, lambda i: 1 + i // tiles_s, state_delta, True, wts, consts, final_g)
    return y_prompt, y_sample, jnp.stack(st, axis=1)
```

```python
import functools

import numpy as np
import jax
import jax.numpy as jnp
from jax import lax
from jax.experimental import pallas as pl
from jax.experimental.pallas import tpu as pltpu

F32 = jnp.float32
BF16 = jnp.bfloat16

W_BR = 512
H_A = 4
DK = 128
CONV_K = 3
CHUNK = 64
PAIR = 2 * CHUNK
G_B = 4
CHUNK_B = 128
G_C = 4
CH_C = W_BR // G_C
GRID_W = 64
COLS_SLAB = 16
EPS = 1e-6
C_BD = 3 * W_BR
C_REST = C_BD + 128
TM = 256
TMM = 512
TD = 1024
LOCKSTEP = 2
HALO = 16
VMEM_LIMIT = 56 * 1024 * 1024


def _bdot(a, b):
    return jnp.dot(a.astype(BF16), b.astype(BF16), preferred_element_type=F32)


def _split(x):
    hi = x.astype(BF16)
    lo = (x - hi.astype(F32)).astype(BF16)
    return hi, lo


def _sigmoid(x):
    return 0.5 * jnp.tanh(0.5 * x) + 0.5


def _silu(x):
    return x * _sigmoid(x)


def _cparams(sem):
    return pltpu.CompilerParams(dimension_semantics=sem, vmem_limit_bytes=VMEM_LIMIT)


def _layer_spec(a, l):
    return pl.BlockSpec((None,) + a.shape[1:], lambda *_: (l,) + (0,) * (a.ndim - 1))


def _full_spec(a):
    return pl.BlockSpec(a.shape, lambda *_: (0,) * a.ndim)


def _ada_kernel(c_ref, w_ref, b_ref, o_ref):
    o_ref[...] = _bdot(_silu(c_ref[...]), w_ref[...]) + b_ref[...]


def _ada_call(cvecs, w_ada, b_ada):
    depth, d, e = w_ada.shape
    r = cvecs.shape[0]
    tn = 1024
    return pl.pallas_call(
        _ada_kernel,
        out_shape=jax.ShapeDtypeStruct((depth, r, e), F32),
        grid=(depth, e // tn),
        in_specs=[pl.BlockSpec((r, d), lambda l, n: (0, 0)),
                  pl.BlockSpec((None, d, tn), lambda l, n: (l, 0, n)),
                  pl.BlockSpec((None, 1, tn), lambda l, n: (l, 0, n))],
        out_specs=pl.BlockSpec((None, r, tn), lambda l, n: (l, 0, n)),
        compiler_params=_cparams(("arbitrary", "arbitrary")),
        name="ada_mod",
    )(cvecs, w_ada, b_ada.reshape(depth, 1, e))


def _regroup_kernel(w_ref, o_ref):
    nq, nbd = C_BD, 4 * H_A
    o_ref[:, :nq] = w_ref[:, :nq].astype(BF16)
    o_ref[:, nq:C_REST] = jnp.concatenate(
        [w_ref[:, nq:nq + nbd], jnp.zeros((w_ref.shape[0], C_REST - nq - nbd), F32)], axis=1).astype(BF16)
    o_ref[:, C_REST:] = w_ref[:, nq + nbd:].astype(BF16)


def _regroup_call(w_in):
    depth, d, n = w_in.shape
    n_out = n + C_REST - C_BD - 4 * H_A
    rows = 256
    return pl.pallas_call(
        _regroup_kernel, out_shape=jax.ShapeDtypeStruct((depth, d, n_out), BF16), grid=(depth, d // rows),
        in_specs=[pl.BlockSpec((None, rows, n), lambda l, r: (l, r, 0))],
        out_specs=pl.BlockSpec((None, rows, n_out), lambda l, r: (l, r, 0)),
        compiler_params=_cparams(("arbitrary", "arbitrary")), name="regroup_w_in",
    )(w_in)


def _in_kernel(x_ref, xp_ref, xn_ref, shift_ref, scale_ref, lng_ref, win_ref,
               convw_ref, adt_ref, sgug_ref, wsp_ref, bsp_ref, cch_ref, mc_ref, ltri_ref, utri_ref,
               q_ref, k_ref, v_ref, bd_ref, bdt_ref, sza_ref, osgu_ref, re_ref, im_ref, szc_ref, sg_ref,
               *, seq_tiles, grid_cols):
    i = pl.program_id(0)
    t_in_seq = i % seq_tiles
    lng = lng_ref[...]
    scale1 = 1.0 + scale_ref[...]
    shift = shift_ref[...]

    x = jnp.concatenate([xp_ref[...], x_ref[...], xn_ref[...]], axis=0)
    y = x * lax.rsqrt(jnp.mean(x * x, axis=-1, keepdims=True) + EPS)
    xe = ((y * lng) * scale1 + shift).astype(BF16)
    xn = xe[HALO:HALO + TM]

    def rest(j):
        return jnp.dot(xn, win_ref[:, C_REST + j * W_BR:C_REST + (j + 1) * W_BR], preferred_element_type=F32)

    pe = jnp.dot(xe, win_ref[:, :C_BD], preferred_element_type=F32)
    pbd = jnp.dot(xn, win_ref[:, C_BD:C_REST], preferred_element_type=F32)
    z_a = rest(0)

    p_prev = jnp.where(t_in_seq == 0, 0.0, pe[HALO - 1:HALO])
    p_next = jnp.where(t_in_seq == seq_tiles - 1, 0.0, pe[HALO + TM:HALO + TM + 1])
    row = lax.broadcasted_iota(jnp.int32, (TM, 1), 0)
    cw = convw_ref[...]

    def conv_silu(c0):
        cs = slice(c0, c0 + DK)
        p = pe[HALO:HALO + TM, cs]
        down = jnp.where(row == 0, p_prev[:, cs], pltpu.roll(p, 1, axis=0))
        up = jnp.where(row == TM - 1, p_next[:, cs], pltpu.roll(p, TM - 1, axis=0))
        return _silu(cw[0:1, cs] * down + cw[1:2, cs] * p + cw[2:3, cs] * up)

    def qkv_head(h):
        qh = conv_silu(h * DK)
        kh = conv_silu(W_BR + h * DK)
        q_ref[h] = qh * lax.rsqrt(jnp.sum(qh * qh, axis=-1, keepdims=True) + EPS) * (DK ** -0.5)
        k_ref[h] = kh * lax.rsqrt(jnp.sum(kh * kh, axis=-1, keepdims=True) + EPS)
        v_ref[h] = conv_silu(2 * W_BR + h * DK)

    def gate_piece(m):
        return jnp.dot(xn, win_ref[:, C_REST + (6 + m) * W_BR:C_REST + (7 + m) * W_BR], preferred_element_type=F32)

    def store_gate(m, gp):
        sg_ref[m // 2, :, (m % 2) * W_BR:(m % 2 + 1) * W_BR] = _sigmoid(gp).astype(BF16)

    u_b = rest(1)
    qkv_head(0)
    v_b = rest(2)
    qkv_head(1)
    z_b = rest(3)
    qkv_head(2)
    x_c = rest(4).astype(BF16)
    qkv_head(3)
    z_c = rest(5)

    adt = adt_ref[...]
    sp_in = pbd + adt[1:2]
    softplus = jnp.maximum(sp_in, 0.0) + jnp.log1p(jnp.exp(-jnp.abs(sp_in)))
    g = -jnp.exp(adt[0:1]) * softplus
    lane = lax.broadcasted_iota(jnp.int32, (TM, 128), 1)
    g = jnp.where((lane >= 2 * H_A) & (lane < 4 * H_A), g, 0.0)
    g1 = g.astype(BF16).astype(F32)
    g2 = (g - g1).astype(BF16).astype(F32)
    g3 = (g - g1 - g2).astype(BF16).astype(F32)
    packed = (g1 + pltpu.roll(g2, 2 * H_A, axis=1) + pltpu.roll(g3, 4 * H_A, axis=1)).astype(BF16)

    def unpack(r):
        return r + pltpu.roll(r, 128 - 2 * H_A, axis=1) + pltpu.roll(r, 128 - 4 * H_A, axis=1)

    gpre = unpack(jnp.dot(ltri_ref[...], packed, preferred_element_type=F32))
    gsuf = unpack(jnp.dot(utri_ref[...], packed, preferred_element_type=F32))
    gam = jnp.where(lane < 3 * H_A, gpre, gsuf)
    bd_ref[...] = jnp.where(lane < 2 * H_A, _sigmoid(pbd), gam)
    gam_t = jnp.transpose(gam)
    for c in range(TM // CHUNK):
        for dp in range(H_A):
            for e in range(2):
                r = 2 * H_A + 2 * dp + e
                bdt_ref[c, dp:dp + 1, e * CHUNK:(e + 1) * CHUNK] = gam_t[r:r + 1, c * CHUNK:(c + 1) * CHUNK]
    sza_ref[...] = _silu(z_a).astype(BF16)

    vn = (v_b * lax.rsqrt(jnp.mean(v_b * v_b, axis=-1, keepdims=True) + EPS) * sgug_ref[...]).astype(BF16)
    gate_b = _silu(z_b)

    def sgu_chunk(n):
        rs = slice(n * CHUNK_B, (n + 1) * CHUNK_B)
        for gi in range(G_B):
            cs = slice(gi * (W_BR // G_B), (gi + 1) * (W_BR // G_B))
            vs = jnp.dot(wsp_ref[gi], vn[rs, cs], preferred_element_type=F32) + bsp_ref[gi]
            osgu_ref[rs, cs] = ((u_b[rs, cs] * vs) * gate_b[rs, cs]).astype(BF16)

    gps = [gate_piece(0)]
    for n in range(TM // CHUNK_B):
        sgu_chunk(n)
        gps.append(gate_piece(len(gps)))

    szc_ref[...] = _silu(z_c).astype(BF16)
    cch = cch_ref[...]
    a_parts, b_parts = [], []
    for gi in range(G_C):
        ab = jnp.dot(x_c[:, gi * CH_C:(gi + 1) * CH_C], cch, preferred_element_type=F32)
        a_parts.append(ab[:, :CH_C])
        b_parts.append(ab[:, CH_C:])
    a = jnp.concatenate(a_parts, axis=1)
    b = jnp.concatenate(b_parts, axis=1)
    gps.append(gate_piece(len(gps)))
    if grid_cols:
        mc = mc_ref[...]
        for r in range(TM // GRID_W):
            rs = slice(r * GRID_W, (r + 1) * GRID_W)
            stacked = jnp.concatenate([a[rs], b[rs]], axis=0).astype(BF16)
            z = jnp.dot(mc, stacked, preferred_element_type=F32)
            re_ref[rs, :] = z[:GRID_W].astype(BF16)
            im_ref[rs, :] = z[GRID_W:].astype(BF16)
    else:
        re_ref[...] = a.astype(BF16)
        im_ref[...] = (-b).astype(BF16)

    for m in range(6):
        if len(gps) < 6:
            gps.append(gate_piece(len(gps)))
        store_gate(m, gps[m])


def _in_call(h, shift, scale, row_of_tile, seq_tiles, grid_cols, l, wts, consts):
    ntok, d = h.shape
    nt = ntok // TM
    nhb = ntok // HALO
    tile = lambda w: pl.BlockSpec((TM, w), lambda i: (i, 0))
    mod = pl.BlockSpec((None, 1, d), lambda i: (row_of_tile(i), 0, 0))
    lws = [wts[n] for n in ("ln_g", "w_in", "conv_w", "adt", "sgu_g", "w_sp", "b_sp")]
    cs = [consts[n] for n in ("cch", "mc", "ltri", "utri")]
    in_specs = [tile(d),
                pl.BlockSpec((HALO, d), lambda i: (jnp.maximum(i * (TM // HALO) - 1, 0), 0)),
                pl.BlockSpec((HALO, d), lambda i: (jnp.minimum((i + 1) * (TM // HALO), nhb - 1), 0)),
                mod, mod] + [_layer_spec(a, l) for a in lws] + [_full_spec(a) for a in cs]
    hsplit = pl.BlockSpec((H_A, TM, DK), lambda i: (0, i, 0))
    out_shape = [jax.ShapeDtypeStruct((H_A, ntok, DK), F32)] * 3 + [
        jax.ShapeDtypeStruct((ntok, 128), F32),
        jax.ShapeDtypeStruct((ntok // CHUNK, H_A, PAIR), F32),
        jax.ShapeDtypeStruct((ntok, W_BR), BF16),
        jax.ShapeDtypeStruct((ntok, W_BR), BF16),
        jax.ShapeDtypeStruct((ntok, W_BR), BF16),
        jax.ShapeDtypeStruct((ntok, W_BR), BF16),
        jax.ShapeDtypeStruct((ntok, W_BR), BF16),
        jax.ShapeDtypeStruct((3, ntok, 2 * W_BR), BF16),
    ]
    out_specs = [hsplit] * 3 + [tile(128), pl.BlockSpec((TM // CHUNK, H_A, PAIR), lambda i: (i, 0, 0)),
                                tile(W_BR), tile(W_BR), tile(W_BR), tile(W_BR), tile(W_BR),
                                pl.BlockSpec((3, TM, 2 * W_BR), lambda i: (0, i, 0))]
    return pl.pallas_call(
        functools.partial(_in_kernel, seq_tiles=seq_tiles, grid_cols=grid_cols),
        out_shape=out_shape, grid=(nt,), in_specs=in_specs, out_specs=out_specs,
        compiler_params=_cparams(("arbitrary",)), name="in_proj",
    )(h, h, h, shift, scale, *lws, *cs)


def _dot3(x, y):
    xh, xl = _split(x)
    yh, yl = _split(y)
    n = y.shape[1]
    rhs = jnp.concatenate([jnp.concatenate([yh, yl], axis=1),
                           jnp.concatenate([yh, jnp.zeros_like(yh)], axis=1)], axis=0)
    z = jnp.dot(jnp.concatenate([xh, xl], axis=1), rhs, preferred_element_type=F32)
    return z[:, :n] + z[:, n:]


def _delta_kernel(qf_ref, kf_ref, vf_ref, bdf_ref, bdtf_ref, qb_ref, kb_ref, vb_ref, bdb_ref, bdtb_ref, s0_ref,
                  of_ref, ob_ref, s_ref):
    j = pl.program_id(1)

    @pl.when(j == 0)
    def _():
        s_ref[...] = s0_ref[...]

    nch = qf_ref.shape[1] // CHUNK
    ri = lax.broadcasted_iota(jnp.int32, (PAIR, PAIR), 0)
    ci = lax.broadcasted_iota(jnp.int32, (PAIR, PAIR), 1)
    same = (ri < CHUNK) == (ci < CHUNK)
    eye = (ri == ci).astype(F32)
    masks = (same & (ri > ci), same & (ri < ci))
    top = lax.broadcasted_iota(jnp.int32, (PAIR, 1), 0) < CHUNK
    refs = ((qf_ref, kf_ref, vf_ref, bdf_ref, bdtf_ref, of_ref), (qb_ref, kb_ref, vb_ref, bdb_ref, bdtb_ref, ob_ref))
    tr_rhs = (((1,), (1,)), ((), ()))
    tr_lhs = (((0,), (0,)), ((), ()))
    dps = [(d, p) for d in range(2) for p in range(H_A // 2)]

    def blockdiag(x):
        return jnp.concatenate([jnp.where(top, x, 0.0), jnp.where(top, 0.0, x)], axis=1)

    def local(steps):
        probs = [(s, d, p) for s in steps for d, p in dps]
        n = len(probs)
        ld = []
        for s, d, p in probs:
            q_ref, k_ref, v_ref, bd_ref, bdt_ref, _ = refs[d]
            c = s if d == 0 else nch - 1 - s
            sl = slice(c * CHUNK, (c + 1) * CHUNK)
            pair = lambda r: jnp.concatenate([r[2 * p, sl, :], r[2 * p + 1, sl, :]], axis=0)
            col = d * H_A + 2 * p
            colpair = lambda c0: jnp.concatenate([bd_ref[sl, c0:c0 + 1], bd_ref[sl, c0 + 1:c0 + 2]], axis=0)
            gam = colpair(2 * H_A + col)
            last = CHUNK - 1 if d == 0 else 0
            tots = [gam[e * CHUNK + last:e * CHUNK + last + 1] for e in range(2)]
            ld.append(dict(q=pair(q_ref), k=pair(k_ref), v=pair(v_ref), beta=colpair(col), gam=gam,
                           gam_row=bdt_ref[c, 2 * d + p:2 * d + p + 1, :], tots=tots, sl=sl))
        yield
        qkk, dec = [], []
        for i, (s, d, p) in enumerate(probs):
            x = ld[i]
            kb = x["k"].astype(BF16)
            qkk.append(lax.dot_general(jnp.concatenate([x["q"].astype(BF16), kb], axis=0), kb, tr_rhs,
                                       preferred_element_type=F32))
            m = masks[d]
            dec.append(jnp.where(m, jnp.exp(jnp.where(m, x["gam"] - x["gam_row"], 0.0)), 0.0))
        yield
        pw = [-((ld[i]["beta"] * qkk[i][PAIR:]) * dec[i]) for i in range(n)]
        t = [eye + pw[i] for i in range(n)]
        pw = [_dot3(pw[i], pw[i]) for i in range(n)]
        yield
        for _ in range(4):
            tp = [_dot3(jnp.concatenate([t[i], pw[i]], axis=0), pw[i]) for i in range(n)]
            t = [t[i] + tp[i][:PAIR] for i in range(n)]
            pw = [tp[i][PAIR:] for i in range(n)]
            yield
        t = [t[i] + _dot3(t[i], pw[i]) for i in range(n)]
        yield
        eg = [jnp.exp(ld[i]["gam"]) for i in range(n)]
        sol = [_dot3(t[i], jnp.concatenate([ld[i]["v"] * ld[i]["beta"], ld[i]["k"] * (ld[i]["beta"] * eg[i])], axis=1))
               for i in range(n)]
        yield
        out = {}
        for i, (s, d, p) in enumerate(probs):
            x = ld[i]
            tot = jnp.concatenate([jnp.broadcast_to(tt, (CHUNK, 1)) for tt in x["tots"]], axis=0)
            out[s, d, p] = dict(
                lhs=jnp.concatenate([blockdiag(sol[i][:, DK:]), blockdiag(x["q"] * eg[i])], axis=0).astype(BF16),
                u0=sol[i][:, :DK],
                qk=(qkk[i][:PAIR] * (dec[i] + eye)).astype(BF16),
                kd=blockdiag(x["k"] * jnp.exp(tot - x["gam"])).astype(BF16),
                gl=jnp.concatenate([jnp.broadcast_to(jnp.exp(tt), (DK, 1)) for tt in x["tots"]], axis=0),
                sl=x["sl"])
        return out

    st = {dp: s_ref[dp[0], 2 * dp[1]:2 * dp[1] + 2].reshape(2 * DK, DK) for dp in dps}

    def scan(loc, steps):
        for s in steps:
            wq = {dp: jnp.dot(loc[(s,) + dp]["lhs"], st[dp].astype(BF16), preferred_element_type=F32) for dp in dps}
            yield
            for d, p in dps:
                x = loc[s, d, p]
                ub = (x["u0"] - wq[d, p][:PAIR]).astype(BF16)
                o = wq[d, p][PAIR:] + jnp.dot(x["qk"], ub, preferred_element_type=F32)
                st[d, p] = x["gl"] * st[d, p] + lax.dot_general(x["kd"], ub, tr_lhs, preferred_element_type=F32)
                for e in range(2):
                    hh = 2 * p + e
                    refs[d][5][x["sl"], hh * DK:(hh + 1) * DK] = o[e * CHUNK:(e + 1) * CHUNK]
            yield

    def run(gen, filler):
        while True:
            try:
                next(gen)
            except StopIteration as done:
                return done.value
            if filler is not None:
                next(filler, None)

    groups = [list(range(g, min(g + LOCKSTEP, nch))) for g in range(0, nch, LOCKSTEP)]
    filler = None
    for steps in groups:
        loc = run(local(steps), filler)
        if filler is not None:
            for _ in filler:
                pass
        filler = scan(loc, steps)
    for _ in filler:
        pass
    for d, p in dps:
        s_ref[d, 2 * p:2 * p + 2] = st[d, p].reshape(2, DK, DK)


def _delta_call(q, k, v, bd, bdt, s0, l, seq):
    ntok = bd.shape[0]
    td = min(TD, seq)
    nb = seq // td
    bsz = ntok // seq
    nch = td // CHUNK
    fw = lambda b, j: b * nb + j
    bw = lambda b, j: b * nb + nb - 1 - j

    def specs(t):
        hs = pl.BlockSpec((H_A, td, DK), lambda b, j: (0, t(b, j), 0))
        return [hs, hs, hs, pl.BlockSpec((td, 128), lambda b, j: (t(b, j), 0)),
                pl.BlockSpec((nch, H_A, PAIR), lambda b, j: (t(b, j), 0, 0))]

    return pl.pallas_call(
        _delta_kernel,
        out_shape=[jax.ShapeDtypeStruct((ntok, W_BR), F32), jax.ShapeDtypeStruct((ntok, W_BR), F32),
                   jax.ShapeDtypeStruct((bsz, 2, H_A, DK, DK), F32)],
        grid=(bsz, nb),
        in_specs=specs(fw) + specs(bw) + [pl.BlockSpec((None, None, 2, H_A, DK, DK),
                                                       lambda b, j: (b, l, 0, 0, 0, 0))],
        out_specs=[pl.BlockSpec((td, W_BR), lambda b, j: (fw(b, j), 0)),
                   pl.BlockSpec((td, W_BR), lambda b, j: (bw(b, j), 0)),
                   pl.BlockSpec((None, 2, H_A, DK, DK), lambda b, j: (b, 0, 0, 0, 0))],
        compiler_params=_cparams(("arbitrary", "arbitrary")), name="delta_scan",
    )(q, k, v, bd, bdt, q, k, v, bd, bdt, s0)


def _fourier_kernel(cr_ref, sr_ref, re_ref, im_ref, szc_ref, o_ref):
    y = (jnp.dot(cr_ref[...], re_ref[...], preferred_element_type=F32)
         + jnp.dot(sr_ref[...], im_ref[...], preferred_element_type=F32))
    o_ref[...] = (y * szc_ref[...]).astype(BF16)


def _fourier_call(cr, sr, re, im, szc, tn):
    bsz, r, n = re.shape
    blk = pl.BlockSpec((None, r, tn), lambda b, c: (b, 0, c))
    mat = pl.BlockSpec((r, r), lambda b, c: (0, 0))
    return pl.pallas_call(
        _fourier_kernel, out_shape=jax.ShapeDtypeStruct((bsz, r, n), BF16),
        grid=(bsz, n // tn), in_specs=[mat, mat, blk, blk, blk], out_specs=blk,
        compiler_params=_cparams(("arbitrary", "arbitrary")), name="fourier_rows",
    )(cr, sr, re, im, szc)


def _fourier_grid_kernel(cr_ref, sr_ref, re_ref, im_ref, szc_ref, o_ref):
    r, s, w = re_ref.shape
    y = (jnp.dot(cr_ref[...], re_ref[...].reshape(r * s, w), preferred_element_type=F32)
         + jnp.dot(sr_ref[...], im_ref[...].reshape(r * s, w), preferred_element_type=F32))
    o_ref[...] = (y * szc_ref[...].reshape(r * s, w)).astype(BF16).reshape(r, s, w)


def _fourier_grid_call(crk, srk, re, im, szc):
    bsz, r, gw, w = re.shape
    blk = pl.BlockSpec((None, r, COLS_SLAB, w), lambda b, c: (b, 0, c, 0))
    mat = pl.BlockSpec(crk.shape, lambda b, c: (0, 0))
    return pl.pallas_call(
        _fourier_grid_kernel, out_shape=jax.ShapeDtypeStruct((bsz, r, gw, w), BF16),
        grid=(bsz, gw // COLS_SLAB), in_specs=[mat, mat, blk, blk, blk], out_specs=blk,
        compiler_params=_cparams(("arbitrary", "arbitrary")), name="fourier_grid_rows",
    )(crk, srk, re, im, szc)


def _merge_kernel(h_ref, of_ref, ob_ref, sza_ref, osgu_ref, oc_ref, sg_ref, gate_ref, ong_ref, wbr_ref, wout_ref,
                  fing_ref, o_ref, *, final):
    o = of_ref[...] + ob_ref[...]
    sza = sza_ref[...]
    ong = ong_ref[...]
    parts = []
    for hh in range(H_A):
        oh = o[:, hh * DK:(hh + 1) * DK]
        oh = (oh * lax.rsqrt(jnp.mean(oh * oh, axis=-1, keepdims=True) + EPS)) * ong
        parts.append((oh * sza[:, hh * DK:(hh + 1) * DK]).astype(BF16))
    o_a = jnp.concatenate(parts, axis=1)
    merged = (sg_ref[0] * jnp.dot(o_a, wbr_ref[0], preferred_element_type=F32)
              + sg_ref[1] * jnp.dot(osgu_ref[...], wbr_ref[1], preferred_element_type=F32)
              + sg_ref[2] * jnp.dot(oc_ref[...], wbr_ref[2], preferred_element_type=F32))
    out = jnp.dot(merged.astype(BF16), wout_ref[...], preferred_element_type=F32)
    hn = h_ref[...] + gate_ref[...] * out
    if final:
        hn = (hn * lax.rsqrt(jnp.mean(hn * hn, axis=-1, keepdims=True) + EPS)) * fing_ref[...]
    o_ref[...] = hn


def _merge_call(h, o_f, o_b, sza, osgu, oc, sg, gate, row_of_tile, l, wts, final_g, final, tm):
    ntok, d = h.shape
    tile = lambda w: pl.BlockSpec((tm, w), lambda i: (i, 0))
    lws = [wts[n] for n in ("o_norm_g", "w_branch", "w_out")]
    in_specs = [tile(d), tile(W_BR), tile(W_BR), tile(W_BR), tile(W_BR), tile(W_BR),
                pl.BlockSpec((3, tm, d), lambda i: (0, i, 0)),
                pl.BlockSpec((None, 1, d), lambda i: (row_of_tile(i * (tm // TM)), 0, 0))
                ] + [_layer_spec(a, l) for a in lws] + [_full_spec(final_g)]
    return pl.pallas_call(
        functools.partial(_merge_kernel, final=final),
        out_shape=jax.ShapeDtypeStruct((ntok, d), F32), grid=(ntok // tm,),
        in_specs=in_specs, out_specs=tile(d),
        compiler_params=_cparams(("arbitrary",)), name="merge_out",
    )(h, o_f, o_b, sza, osgu, oc, sg, gate, *lws, final_g)


def _dft(n):
    idx = np.arange(n)
    ang = 2.0 * np.pi * ((idx[:, None] * idx[None, :]) % n) / n
    return np.cos(ang) / np.sqrt(n), np.sin(ang) / np.sqrt(n)


def _consts(seq, rows):
    cc, sc = _dft(CH_C)
    cg, sg = _dft(GRID_W)
    cp, sp = _dft(seq)
    cr, sr = _dft(rows)
    blk = np.kron(np.eye(TM // CHUNK), np.tril(np.ones((CHUNK, CHUNK))))
    bf = lambda a: jnp.asarray(a, F32).astype(BF16)
    return {
        "cch": bf(np.concatenate([cc, sc], axis=1)),
        "mc": bf(np.block([[cg, -sg], [-sg, -cg]])),
        "ltri": bf(blk), "utri": bf(blk.T),
        "cp": bf(cp), "sp": bf(sp),
        "crk": bf(np.kron(cr, np.eye(COLS_SLAB))), "srk": bf(np.kron(sr, np.eye(COLS_SLAB))),
    }


def _stacked_weights(w_in, ln_g, conv_w, a_log, dt_bias, o_norm_g, sgu_norm_g, w_spatial, b_spatial, w_branch, w_out):
    depth, d, _ = w_in.shape
    nbd = 4 * H_A
    adt = jnp.stack([a_log.reshape(depth, -1), dt_bias.reshape(depth, -1)], axis=1)
    return {
        "ln_g": ln_g.reshape(depth, 1, d),
        "w_in": _regroup_call(w_in),
        "conv_w": conv_w,
        "adt": jnp.pad(adt, ((0, 0), (0, 0), (2 * H_A, 128 - nbd))),
        "sgu_g": sgu_norm_g.reshape(depth, 1, W_BR),
        "w_sp": w_spatial.astype(BF16),
        "b_sp": jnp.broadcast_to(b_spatial[..., None], b_spatial.shape + (W_BR // G_B,)),
        "o_norm_g": o_norm_g.reshape(depth, 1, DK),
        "w_branch": w_branch.astype(BF16),
        "w_out": w_out.astype(BF16),
    }


def _group_forward(x, mods, row_of_tile, s0_all, grid_cols, wts, consts, final_g):
    bsz, t, d = x.shape
    ntok = bsz * t
    nb = t // TM
    h = x.reshape(ntok, d)
    tm_merge = TMM if (t % TMM == 0 or not grid_cols) and ntok % TMM == 0 else TM
    states = []
    depth = mods.shape[0]
    for l in range(depth):
        shift, scale, gate = mods[l, 0], mods[l, 1], mods[l, 2]
        q, k, v, bd, bdt, sza, osgu, re, im, szc, sg = _in_call(h, shift, scale, row_of_tile, nb, grid_cols, l, wts, consts)
        o_f, o_b, s_fin = _delta_call(q, k, v, bd, bdt, s0_all, min(l, s0_all.shape[1] - 1), t)
        states.append(s_fin)
        if grid_cols:
            shp = (bsz, t // GRID_W, GRID_W, W_BR)
            oc = _fourier_grid_call(consts["crk"], consts["srk"], re.reshape(shp), im.reshape(shp), szc.reshape(shp))
        else:
            shp = (bsz, t, W_BR)
            oc = _fourier_call(consts["cp"], consts["sp"], re.reshape(shp), im.reshape(shp), szc.reshape(shp), W_BR)
        h = _merge_call(h, o_f, o_b, sza, osgu, oc.reshape(ntok, W_BR), sg, gate, row_of_tile, l, wts,
                        final_g.reshape(1, d), l == depth - 1, tm_merge)
    return h.reshape(bsz, t, d), states


def kernel(x_prompt, x_sample, state_delta, c, c_ctx, ln_g, w_ada, b_ada, w_in, conv_w, a_log, dt_bias, o_norm_g,
           sgu_norm_g, w_spatial, b_spatial, w_branch, w_out, final_g):
    depth, d = ln_g.shape
    bp, tp, _ = x_prompt.shape
    bs, ts, _ = x_sample.shape
    assert conv_w.shape[1:] == (CONV_K, 3 * W_BR) and w_in.shape[2] == 9 * W_BR + 4 * H_A + 3 * d
    assert w_spatial.shape[1:] == (G_B, CHUNK_B, CHUNK_B) and state_delta.shape[2:] == (2, H_A, DK, DK)
    assert all(t % TM == 0 and (t % TD == 0 or TD % t == 0) for t in (tp, ts))
    assert TM % GRID_W == 0 and GRID_W % COLS_SLAB == 0
    consts = _consts(tp, ts // GRID_W)
    wts = _stacked_weights(w_in, ln_g, conv_w, a_log, dt_bias, o_norm_g, sgu_norm_g, w_spatial, b_spatial,
                           w_branch, w_out)
    nrow = -(-(1 + bs) // 8) * 8
    cvecs = jnp.concatenate([c_ctx[None], c, jnp.zeros((nrow - 1 - bs, d), F32)], axis=0)
    mods = _ada_call(cvecs, w_ada, b_ada).reshape(depth, nrow, 3, 1, d).transpose(0, 2, 1, 3, 4)

    zeros = jnp.zeros((bp, 1, 2, H_A, DK, DK), F32)
    y_prompt, st = _group_forward(x_prompt, mods, lambda i: 0, zeros, False, wts, consts, final_g)
    tiles_s = ts // TM
    y_sample, _ = _group_forward(x_sample, mods, lambda i: 1 + i // tiles_s, state_delta, True, wts, consts, final_g)
    return y_prompt, y_sample, jnp.stack(st, axis=1)
```

```python
import functools

import numpy as np
import jax
import jax.numpy as jnp
from jax import lax
from jax.experimental import pallas as pl
from jax.experimental.pallas import tpu as pltpu

F32 = jnp.float32
BF16 = jnp.bfloat16

W_BR = 512
H_A = 4
DK = 128
CONV_K = 3
CHUNK = 64
PAIR = 2 * CHUNK
G_B = 4
CHUNK_B = 128
G_C = 4
CH_C = W_BR // G_C
GRID_W = 64
COLS_SLAB = 16
EPS = 1e-6
C_BD = 3 * W_BR
C_REST = C_BD + 128
TM = 256
SUB = 2
N_STAGES = 8
TAIL = 2
TMM = 512
TD = 1024
LOCKSTEP = 2
HALO = 16
VMEM_LIMIT = 56 * 1024 * 1024


def _bdot(a, b):
    return jnp.dot(a.astype(BF16), b.astype(BF16), preferred_element_type=F32)


def _split(x):
    hi = x.astype(BF16)
    lo = (x - hi.astype(F32)).astype(BF16)
    return hi, lo


def _sigmoid(x):
    return 0.5 * jnp.tanh(0.5 * x) + 0.5


def _silu(x):
    return x * _sigmoid(x)


def _cparams(sem):
    return pltpu.CompilerParams(dimension_semantics=sem, vmem_limit_bytes=VMEM_LIMIT)


def _layer_spec(a, l):
    return pl.BlockSpec((None,) + a.shape[1:], lambda *_: (l,) + (0,) * (a.ndim - 1))


def _full_spec(a):
    return pl.BlockSpec(a.shape, lambda *_: (0,) * a.ndim)


def _ada_kernel(c_ref, w_ref, b_ref, o_ref):
    o_ref[...] = _bdot(_silu(c_ref[...]), w_ref[...]) + b_ref[...]


def _ada_call(cvecs, w_ada, b_ada):
    depth, d, e = w_ada.shape
    r = cvecs.shape[0]
    tn = 1024
    return pl.pallas_call(
        _ada_kernel,
        out_shape=jax.ShapeDtypeStruct((depth, r, e), F32),
        grid=(depth, e // tn),
        in_specs=[pl.BlockSpec((r, d), lambda l, n: (0, 0)),
                  pl.BlockSpec((None, d, tn), lambda l, n: (l, 0, n)),
                  pl.BlockSpec((None, 1, tn), lambda l, n: (l, 0, n))],
        out_specs=pl.BlockSpec((None, r, tn), lambda l, n: (l, 0, n)),
        compiler_params=_cparams(("arbitrary", "arbitrary")),
        name="ada_mod",
    )(cvecs, w_ada, b_ada.reshape(depth, 1, e))


def _regroup_kernel(w_ref, o_ref):
    nq, nbd = C_BD, 4 * H_A
    o_ref[:, :nq] = w_ref[:, :nq].astype(BF16)
    o_ref[:, nq:C_REST] = jnp.concatenate(
        [w_ref[:, nq:nq + nbd], jnp.zeros((w_ref.shape[0], C_REST - nq - nbd), F32)], axis=1).astype(BF16)
    o_ref[:, C_REST:] = w_ref[:, nq + nbd:].astype(BF16)


def _regroup_call(w_in):
    depth, d, n = w_in.shape
    n_out = n + C_REST - C_BD - 4 * H_A
    rows = 256
    return pl.pallas_call(
        _regroup_kernel, out_shape=jax.ShapeDtypeStruct((depth, d, n_out), BF16), grid=(depth, d // rows),
        in_specs=[pl.BlockSpec((None, rows, n), lambda l, r: (l, r, 0))],
        out_specs=pl.BlockSpec((None, rows, n_out), lambda l, r: (l, r, 0)),
        compiler_params=_cparams(("arbitrary", "arbitrary")), name="regroup_w_in",
    )(w_in)


def _in_kernel(x_ref, xp_ref, xn_ref, shift_ref, scale_ref, lng_ref, win_ref,
               convw_ref, adt_ref, sgug_ref, wsp_ref, bsp_ref, cch_ref, mc_ref, ltri_ref, utri_ref,
               q_ref, k_ref, v_ref, bd_ref, bdt_ref, sza_ref, osgu_ref, re_ref, im_ref, szc_ref, sg_ref,
               *, seq_tiles, grid_cols):
    i = pl.program_id(0)
    lng = lng_ref[...]
    scale1 = 1.0 + scale_ref[...]
    shift = shift_ref[...]
    cw = convw_ref[...]
    adt = adt_ref[...]
    cch = cch_ref[...]
    row = lax.broadcasted_iota(jnp.int32, (TM, 1), 0)
    lane = lax.broadcasted_iota(jnp.int32, (TM, 128), 1)

    def tile(t):
        r0 = t * TM
        rows = slice(r0, r0 + TM)
        t_in_seq = (i * SUB + t) % seq_tiles
        prev = xp_ref[...] if t == 0 else x_ref[r0 - HALO:r0, :]
        nxt = xn_ref[...] if t == SUB - 1 else x_ref[r0 + TM:r0 + TM + HALO, :]
        x = jnp.concatenate([prev, x_ref[rows, :], nxt], axis=0)
        y = x * lax.rsqrt(jnp.mean(x * x, axis=-1, keepdims=True) + EPS)
        xe = ((y * lng) * scale1 + shift).astype(BF16)
        xn = xe[HALO:HALO + TM]

        def rest(j):
            return jnp.dot(xn, win_ref[:, C_REST + j * W_BR:C_REST + (j + 1) * W_BR], preferred_element_type=F32)

        pe = jnp.dot(xe, win_ref[:, :C_BD], preferred_element_type=F32)
        pbd = jnp.dot(xn, win_ref[:, C_BD:C_REST], preferred_element_type=F32)
        z_a = rest(0)
        yield

        p_prev = jnp.where(t_in_seq == 0, 0.0, pe[HALO - 1:HALO])
        p_next = jnp.where(t_in_seq == seq_tiles - 1, 0.0, pe[HALO + TM:HALO + TM + 1])

        def conv_silu(c0):
            cs = slice(c0, c0 + DK)
            p = pe[HALO:HALO + TM, cs]
            down = jnp.where(row == 0, p_prev[:, cs], pltpu.roll(p, 1, axis=0))
            up = jnp.where(row == TM - 1, p_next[:, cs], pltpu.roll(p, TM - 1, axis=0))
            return _silu(cw[0:1, cs] * down + cw[1:2, cs] * p + cw[2:3, cs] * up)

        def qkv_head(h):
            qh = conv_silu(h * DK)
            kh = conv_silu(W_BR + h * DK)
            q_ref[h, rows, :] = qh * lax.rsqrt(jnp.sum(qh * qh, axis=-1, keepdims=True) + EPS) * (DK ** -0.5)
            k_ref[h, rows, :] = kh * lax.rsqrt(jnp.sum(kh * kh, axis=-1, keepdims=True) + EPS)
            v_ref[h, rows, :] = conv_silu(2 * W_BR + h * DK)

        def gate_piece(m):
            return jnp.dot(xn, win_ref[:, C_REST + (6 + m) * W_BR:C_REST + (7 + m) * W_BR],
                           preferred_element_type=F32)

        def store_gate(m, gp):
            sg_ref[m // 2, rows, (m % 2) * W_BR:(m % 2 + 1) * W_BR] = _sigmoid(gp).astype(BF16)

        u_b = rest(1)
        qkv_head(0)
        v_b = rest(2)
        qkv_head(1)
        yield
        z_b = rest(3)
        qkv_head(2)
        x_c = rest(4).astype(BF16)
        qkv_head(3)
        z_c = rest(5)
        yield

        sp_in = pbd + adt[1:2]
        softplus = jnp.maximum(sp_in, 0.0) + jnp.log1p(jnp.exp(-jnp.abs(sp_in)))
        g = -jnp.exp(adt[0:1]) * softplus
        g = jnp.where((lane >= 2 * H_A) & (lane < 4 * H_A), g, 0.0)
        g1 = g.astype(BF16).astype(F32)
        g2 = (g - g1).astype(BF16).astype(F32)
        g3 = (g - g1 - g2).astype(BF16).astype(F32)
        packed = (g1 + pltpu.roll(g2, 2 * H_A, axis=1) + pltpu.roll(g3, 4 * H_A, axis=1)).astype(BF16)

        def unpack(r):
            return r + pltpu.roll(r, 128 - 2 * H_A, axis=1) + pltpu.roll(r, 128 - 4 * H_A, axis=1)

        gpre = unpack(jnp.dot(ltri_ref[...], packed, preferred_element_type=F32))
        gsuf = unpack(jnp.dot(utri_ref[...], packed, preferred_element_type=F32))
        gam = jnp.where(lane < 3 * H_A, gpre, gsuf)
        bd_ref[rows, :] = jnp.where(lane < 2 * H_A, _sigmoid(pbd), gam)
        gam_t = jnp.transpose(gam)
        for c in range(TM // CHUNK):
            for dp in range(H_A):
                for e in range(2):
                    r = 2 * H_A + 2 * dp + e
                    bdt_ref[t * (TM // CHUNK) + c, dp:dp + 1, e * CHUNK:(e + 1) * CHUNK] = (
                        gam_t[r:r + 1, c * CHUNK:(c + 1) * CHUNK])
        sza_ref[rows, :] = _silu(z_a).astype(BF16)

        vn = (v_b * lax.rsqrt(jnp.mean(v_b * v_b, axis=-1, keepdims=True) + EPS) * sgug_ref[...]).astype(BF16)
        gate_b = _silu(z_b)

        def sgu_chunk(n):
            rs = slice(n * CHUNK_B, (n + 1) * CHUNK_B)
            for gi in range(G_B):
                cs = slice(gi * (W_BR // G_B), (gi + 1) * (W_BR // G_B))
                vs = jnp.dot(wsp_ref[gi], vn[rs, cs], preferred_element_type=F32) + bsp_ref[gi]
                osgu_ref[r0 + n * CHUNK_B:r0 + (n + 1) * CHUNK_B, cs] = (
                    (u_b[rs, cs] * vs) * gate_b[rs, cs]).astype(BF16)

        gps = [gate_piece(0)]
        for n in range(TM // CHUNK_B):
            sgu_chunk(n)
            gps.append(gate_piece(len(gps)))
        yield

        szc_ref[rows, :] = _silu(z_c).astype(BF16)
        a_parts, b_parts = [], []
        for gi in range(G_C):
            ab = jnp.dot(x_c[:, gi * CH_C:(gi + 1) * CH_C], cch, preferred_element_type=F32)
            a_parts.append(ab[:, :CH_C])
            b_parts.append(ab[:, CH_C:])
        a = jnp.concatenate(a_parts, axis=1)
        b = jnp.concatenate(b_parts, axis=1)
        gps.append(gate_piece(len(gps)))
        if grid_cols:
            mc = mc_ref[...]
            for r in range(TM // GRID_W):
                rs = slice(r * GRID_W, (r + 1) * GRID_W)
                stacked = jnp.concatenate([a[rs], b[rs]], axis=0).astype(BF16)
                z = jnp.dot(mc, stacked, preferred_element_type=F32)
                re_ref[r0 + r * GRID_W:r0 + (r + 1) * GRID_W, :] = z[:GRID_W].astype(BF16)
                im_ref[r0 + r * GRID_W:r0 + (r + 1) * GRID_W, :] = z[GRID_W:].astype(BF16)
        else:
            re_ref[rows, :] = a.astype(BF16)
            im_ref[rows, :] = (-b).astype(BF16)
        yield

        for m in range(6):
            if len(gps) < 6:
                gps.append(gate_piece(len(gps)))
            store_gate(m, gps[m])
            if m in (1, 3):
                yield

    first, second = (tile(t) for t in range(SUB))
    done = object()
    for _ in range(N_STAGES - TAIL):
        next(first)
    while next(first, done) is not done:
        next(second)
    for _ in second:
        pass


def _in_call(h, shift, scale, row_of_tile, seq_tiles, grid_cols, l, wts, consts):
    ntok, d = h.shape
    tb = SUB * TM
    nt = ntok // tb
    nhb = ntok // HALO
    tile = lambda w: pl.BlockSpec((tb, w), lambda i: (i, 0))
    mod = pl.BlockSpec((None, 1, d), lambda i: (row_of_tile(i * SUB), 0, 0))
    lws = [wts[n] for n in ("ln_g", "w_in", "conv_w", "adt", "sgu_g", "w_sp", "b_sp")]
    cs = [consts[n] for n in ("cch", "mc", "ltri", "utri")]
    in_specs = [tile(d),
                pl.BlockSpec((HALO, d), lambda i: (jnp.maximum(i * (tb // HALO) - 1, 0), 0)),
                pl.BlockSpec((HALO, d), lambda i: (jnp.minimum((i + 1) * (tb // HALO), nhb - 1), 0)),
                mod, mod] + [_layer_spec(a, l) for a in lws] + [_full_spec(a) for a in cs]
    hsplit = pl.BlockSpec((H_A, tb, DK), lambda i: (0, i, 0))
    out_shape = [jax.ShapeDtypeStruct((H_A, ntok, DK), F32)] * 3 + [
        jax.ShapeDtypeStruct((ntok, 128), F32),
        jax.ShapeDtypeStruct((ntok // CHUNK, H_A, PAIR), F32),
        jax.ShapeDtypeStruct((ntok, W_BR), BF16),
        jax.ShapeDtypeStruct((ntok, W_BR), BF16),
        jax.ShapeDtypeStruct((ntok, W_BR), BF16),
        jax.ShapeDtypeStruct((ntok, W_BR), BF16),
        jax.ShapeDtypeStruct((ntok, W_BR), BF16),
        jax.ShapeDtypeStruct((3, ntok, 2 * W_BR), BF16),
    ]
    out_specs = [hsplit] * 3 + [tile(128), pl.BlockSpec((tb // CHUNK, H_A, PAIR), lambda i: (i, 0, 0)),
                                tile(W_BR), tile(W_BR), tile(W_BR), tile(W_BR), tile(W_BR),
                                pl.BlockSpec((3, tb, 2 * W_BR), lambda i: (0, i, 0))]
    return pl.pallas_call(
        functools.partial(_in_kernel, seq_tiles=seq_tiles, grid_cols=grid_cols),
        out_shape=out_shape, grid=(nt,), in_specs=in_specs, out_specs=out_specs,
        compiler_params=_cparams(("arbitrary",)), name="in_proj",
    )(h, h, h, shift, scale, *lws, *cs)


def _dot3(x, y):
    xh, xl = _split(x)
    yh, yl = _split(y)
    n = y.shape[1]
    rhs = jnp.concatenate([jnp.concatenate([yh, yl], axis=1),
                           jnp.concatenate([yh, jnp.zeros_like(yh)], axis=1)], axis=0)
    z = jnp.dot(jnp.concatenate([xh, xl], axis=1), rhs, preferred_element_type=F32)
    return z[:, :n] + z[:, n:]


def _delta_kernel(qf_ref, kf_ref, vf_ref, bdf_ref, bdtf_ref, qb_ref, kb_ref, vb_ref, bdb_ref, bdtb_ref, s0_ref,
                  of_ref, ob_ref, s_ref):
    j = pl.program_id(1)

    @pl.when(j == 0)
    def _():
        s_ref[...] = s0_ref[...]

    nch = qf_ref.shape[1] // CHUNK
    ri = lax.broadcasted_iota(jnp.int32, (PAIR, PAIR), 0)
    ci = lax.broadcasted_iota(jnp.int32, (PAIR, PAIR), 1)
    same = (ri < CHUNK) == (ci < CHUNK)
    eye = (ri == ci).astype(F32)
    masks = (same & (ri > ci), same & (ri < ci))
    top = lax.broadcasted_iota(jnp.int32, (PAIR, 1), 0) < CHUNK
    refs = ((qf_ref, kf_ref, vf_ref, bdf_ref, bdtf_ref, of_ref), (qb_ref, kb_ref, vb_ref, bdb_ref, bdtb_ref, ob_ref))
    tr_rhs = (((1,), (1,)), ((), ()))
    tr_lhs = (((0,), (0,)), ((), ()))
    dps = [(d, p) for d in range(2) for p in range(H_A // 2)]

    def blockdiag(x):
        return jnp.concatenate([jnp.where(top, x, 0.0), jnp.where(top, 0.0, x)], axis=1)

    def local(steps):
        probs = [(s, d, p) for s in steps for d, p in dps]
        n = len(probs)
        ld = []
        for s, d, p in probs:
            q_ref, k_ref, v_ref, bd_ref, bdt_ref, _ = refs[d]
            c = s if d == 0 else nch - 1 - s
            sl = slice(c * CHUNK, (c + 1) * CHUNK)
            pair = lambda r: jnp.concatenate([r[2 * p, sl, :], r[2 * p + 1, sl, :]], axis=0)
            col = d * H_A + 2 * p
            colpair = lambda c0: jnp.concatenate([bd_ref[sl, c0:c0 + 1], bd_ref[sl, c0 + 1:c0 + 2]], axis=0)
            gam = colpair(2 * H_A + col)
            last = CHUNK - 1 if d == 0 else 0
            tots = [gam[e * CHUNK + last:e * CHUNK + last + 1] for e in range(2)]
            ld.append(dict(q=pair(q_ref), k=pair(k_ref), v=pair(v_ref), beta=colpair(col), gam=gam,
                           gam_row=bdt_ref[c, 2 * d + p:2 * d + p + 1, :], tots=tots, sl=sl))
        yield
        qkk, dec = [], []
        for i, (s, d, p) in enumerate(probs):
            x = ld[i]
            kb = x["k"].astype(BF16)
            qkk.append(lax.dot_general(jnp.concatenate([x["q"].astype(BF16), kb], axis=0), kb, tr_rhs,
                                       preferred_element_type=F32))
            m = masks[d]
            dec.append(jnp.where(m, jnp.exp(jnp.where(m, x["gam"] - x["gam_row"], 0.0)), 0.0))
        yield
        pw = [-((ld[i]["beta"] * qkk[i][PAIR:]) * dec[i]) for i in range(n)]
        t = [eye + pw[i] for i in range(n)]
        pw = [_dot3(pw[i], pw[i]) for i in range(n)]
        yield
        for _ in range(4):
            tp = [_dot3(jnp.concatenate([t[i], pw[i]], axis=0), pw[i]) for i in range(n)]
            t = [t[i] + tp[i][:PAIR] for i in range(n)]
            pw = [tp[i][PAIR:] for i in range(n)]
            yield
        t = [t[i] + _dot3(t[i], pw[i]) for i in range(n)]
        yield
        eg = [jnp.exp(ld[i]["gam"]) for i in range(n)]
        sol = [_dot3(t[i], jnp.concatenate([ld[i]["v"] * ld[i]["beta"], ld[i]["k"] * (ld[i]["beta"] * eg[i])], axis=1))
               for i in range(n)]
        yield
        out = {}
        for i, (s, d, p) in enumerate(probs):
            x = ld[i]
            tot = jnp.concatenate([jnp.broadcast_to(tt, (CHUNK, 1)) for tt in x["tots"]], axis=0)
            out[s, d, p] = dict(
                lhs=jnp.concatenate([blockdiag(sol[i][:, DK:]), blockdiag(x["q"] * eg[i])], axis=0).astype(BF16),
                u0=sol[i][:, :DK],
                qk=(qkk[i][:PAIR] * (dec[i] + eye)).astype(BF16),
                kd=blockdiag(x["k"] * jnp.exp(tot - x["gam"])).astype(BF16),
                gl=jnp.concatenate([jnp.broadcast_to(jnp.exp(tt), (DK, 1)) for tt in x["tots"]], axis=0),
                sl=x["sl"])
        return out

    st = {dp: s_ref[dp[0], 2 * dp[1]:2 * dp[1] + 2].reshape(2 * DK, DK) for dp in dps}

    def scan(loc, steps):
        for s in steps:
            wq = {dp: jnp.dot(loc[(s,) + dp]["lhs"], st[dp].astype(BF16), preferred_element_type=F32) for dp in dps}
            yield
            for d, p in dps:
                x = loc[s, d, p]
                ub = (x["u0"] - wq[d, p][:PAIR]).astype(BF16)
                o = wq[d, p][PAIR:] + jnp.dot(x["qk"], ub, preferred_element_type=F32)
                st[d, p] = x["gl"] * st[d, p] + lax.dot_general(x["kd"], ub, tr_lhs, preferred_element_type=F32)
                for e in range(2):
                    hh = 2 * p + e
                    refs[d][5][x["sl"], hh * DK:(hh + 1) * DK] = o[e * CHUNK:(e + 1) * CHUNK]
            yield

    def run(gen, filler):
        while True:
            try:
                next(gen)
            except StopIteration as done:
                return done.value
            if filler is not None:
                next(filler, None)

    groups = [list(range(g, min(g + LOCKSTEP, nch))) for g in range(0, nch, LOCKSTEP)]
    filler = None
    for steps in groups:
        loc = run(local(steps), filler)
        if filler is not None:
            for _ in filler:
                pass
        filler = scan(loc, steps)
    for _ in filler:
        pass
    for d, p in dps:
        s_ref[d, 2 * p:2 * p + 2] = st[d, p].reshape(2, DK, DK)


def _delta_call(q, k, v, bd, bdt, s0, l, seq):
    ntok = bd.shape[0]
    td = min(TD, seq)
    nb = seq // td
    bsz = ntok // seq
    nch = td // CHUNK
    fw = lambda b, j: b * nb + j
    bw = lambda b, j: b * nb + nb - 1 - j

    def specs(t):
        hs = pl.BlockSpec((H_A, td, DK), lambda b, j: (0, t(b, j), 0))
        return [hs, hs, hs, pl.BlockSpec((td, 128), lambda b, j: (t(b, j), 0)),
                pl.BlockSpec((nch, H_A, PAIR), lambda b, j: (t(b, j), 0, 0))]

    return pl.pallas_call(
        _delta_kernel,
        out_shape=[jax.ShapeDtypeStruct((ntok, W_BR), F32), jax.ShapeDtypeStruct((ntok, W_BR), F32),
                   jax.ShapeDtypeStruct((bsz, 2, H_A, DK, DK), F32)],
        grid=(bsz, nb),
        in_specs=specs(fw) + specs(bw) + [pl.BlockSpec((None, None, 2, H_A, DK, DK),
                                                       lambda b, j: (b, l, 0, 0, 0, 0))],
        out_specs=[pl.BlockSpec((td, W_BR), lambda b, j: (fw(b, j), 0)),
                   pl.BlockSpec((td, W_BR), lambda b, j: (bw(b, j), 0)),
                   pl.BlockSpec((None, 2, H_A, DK, DK), lambda b, j: (b, 0, 0, 0, 0))],
        compiler_params=_cparams(("arbitrary", "arbitrary")), name="delta_scan",
    )(q, k, v, bd, bdt, q, k, v, bd, bdt, s0)


def _fourier_kernel(cr_ref, sr_ref, re_ref, im_ref, szc_ref, o_ref):
    y = (jnp.dot(cr_ref[...], re_ref[...], preferred_element_type=F32)
         + jnp.dot(sr_ref[...], im_ref[...], preferred_element_type=F32))
    o_ref[...] = (y * szc_ref[...]).astype(BF16)


def _fourier_call(cr, sr, re, im, szc, tn):
    bsz, r, n = re.shape
    blk = pl.BlockSpec((None, r, tn), lambda b, c: (b, 0, c))
    mat = pl.BlockSpec((r, r), lambda b, c: (0, 0))
    return pl.pallas_call(
        _fourier_kernel, out_shape=jax.ShapeDtypeStruct((bsz, r, n), BF16),
        grid=(bsz, n // tn), in_specs=[mat, mat, blk, blk, blk], out_specs=blk,
        compiler_params=_cparams(("arbitrary", "arbitrary")), name="fourier_rows",
    )(cr, sr, re, im, szc)


def _fourier_grid_kernel(cr_ref, sr_ref, re_ref, im_ref, szc_ref, o_ref):
    r, s, w = re_ref.shape
    y = (jnp.dot(cr_ref[...], re_ref[...].reshape(r * s, w), preferred_element_type=F32)
         + jnp.dot(sr_ref[...], im_ref[...].reshape(r * s, w), preferred_element_type=F32))
    o_ref[...] = (y * szc_ref[...].reshape(r * s, w)).astype(BF16).reshape(r, s, w)


def _fourier_grid_call(crk, srk, re, im, szc):
    bsz, r, gw, w = re.shape
    blk = pl.BlockSpec((None, r, COLS_SLAB, w), lambda b, c: (b, 0, c, 0))
    mat = pl.BlockSpec(crk.shape, lambda b, c: (0, 0))
    return pl.pallas_call(
        _fourier_grid_kernel, out_shape=jax.ShapeDtypeStruct((bsz, r, gw, w), BF16),
        grid=(bsz, gw // COLS_SLAB), in_specs=[mat, mat, blk, blk, blk], out_specs=blk,
        compiler_params=_cparams(("arbitrary", "arbitrary")), name="fourier_grid_rows",
    )(crk, srk, re, im, szc)


def _merge_kernel(h_ref, of_ref, ob_ref, sza_ref, osgu_ref, oc_ref, sg_ref, gate_ref, ong_ref, wbr_ref, wout_ref,
                  fing_ref, o_ref, *, final):
    o = of_ref[...] + ob_ref[...]
    sza = sza_ref[...]
    ong = ong_ref[...]
    parts = []
    for hh in range(H_A):
        oh = o[:, hh * DK:(hh + 1) * DK]
        oh = (oh * lax.rsqrt(jnp.mean(oh * oh, axis=-1, keepdims=True) + EPS)) * ong
        parts.append((oh * sza[:, hh * DK:(hh + 1) * DK]).astype(BF16))
    o_a = jnp.concatenate(parts, axis=1)
    merged = (sg_ref[0] * jnp.dot(o_a, wbr_ref[0], preferred_element_type=F32)
              + sg_ref[1] * jnp.dot(osgu_ref[...], wbr_ref[1], preferred_element_type=F32)
              + sg_ref[2] * jnp.dot(oc_ref[...], wbr_ref[2], preferred_element_type=F32))
    out = jnp.dot(merged.astype(BF16), wout_ref[...], preferred_element_type=F32)
    hn = h_ref[...] + gate_ref[...] * out
    if final:
        hn = (hn * lax.rsqrt(jnp.mean(hn * hn, axis=-1, keepdims=True) + EPS)) * fing_ref[...]
    o_ref[...] = hn


def _merge_call(h, o_f, o_b, sza, osgu, oc, sg, gate, row_of_tile, l, wts, final_g, final, tm):
    ntok, d = h.shape
    tile = lambda w: pl.BlockSpec((tm, w), lambda i: (i, 0))
    lws = [wts[n] for n in ("o_norm_g", "w_branch", "w_out")]
    in_specs = [tile(d), tile(W_BR), tile(W_BR), tile(W_BR), tile(W_BR), tile(W_BR),
                pl.BlockSpec((3, tm, d), lambda i: (0, i, 0)),
                pl.BlockSpec((None, 1, d), lambda i: (row_of_tile(i * (tm // TM)), 0, 0))
                ] + [_layer_spec(a, l) for a in lws] + [_full_spec(final_g)]
    return pl.pallas_call(
        functools.partial(_merge_kernel, final=final),
        out_shape=jax.ShapeDtypeStruct((ntok, d), F32), grid=(ntok // tm,),
        in_specs=in_specs, out_specs=tile(d),
        compiler_params=_cparams(("arbitrary",)), name="merge_out",
    )(h, o_f, o_b, sza, osgu, oc, sg, gate, *lws, final_g)


def _dft(n):
    idx = np.arange(n)
    ang = 2.0 * np.pi * ((idx[:, None] * idx[None, :]) % n) / n
    return np.cos(ang) / np.sqrt(n), np.sin(ang) / np.sqrt(n)


def _consts(seq, rows):
    cc, sc = _dft(CH_C)
    cg, sg = _dft(GRID_W)
    cp, sp = _dft(seq)
    cr, sr = _dft(rows)
    blk = np.kron(np.eye(TM // CHUNK), np.tril(np.ones((CHUNK, CHUNK))))
    bf = lambda a: jnp.asarray(a, F32).astype(BF16)
    return {
        "cch": bf(np.concatenate([cc, sc], axis=1)),
        "mc": bf(np.block([[cg, -sg], [-sg, -cg]])),
        "ltri": bf(blk), "utri": bf(blk.T),
        "cp": bf(cp), "sp": bf(sp),
        "crk": bf(np.kron(cr, np.eye(COLS_SLAB))), "srk": bf(np.kron(sr, np.eye(COLS_SLAB))),
    }


def _stacked_weights(w_in, ln_g, conv_w, a_log, dt_bias, o_norm_g, sgu_norm_g, w_spatial, b_spatial, w_branch, w_out):
    depth, d, _ = w_in.shape
    nbd = 4 * H_A
    adt = jnp.stack([a_log.reshape(depth, -1), dt_bias.reshape(depth, -1)], axis=1)
    return {
        "ln_g": ln_g.reshape(depth, 1, d),
        "w_in": _regroup_call(w_in),
        "conv_w": conv_w,
        "adt": jnp.pad(adt, ((0, 0), (0, 0), (2 * H_A, 128 - nbd))),
        "sgu_g": sgu_norm_g.reshape(depth, 1, W_BR),
        "w_sp": w_spatial.astype(BF16),
        "b_sp": jnp.broadcast_to(b_spatial[..., None], b_spatial.shape + (W_BR // G_B,)),
        "o_norm_g": o_norm_g.reshape(depth, 1, DK),
        "w_branch": w_branch.astype(BF16),
        "w_out": w_out.astype(BF16),
    }


def _group_forward(x, mods, row_of_tile, s0_all, grid_cols, wts, consts, final_g):
    bsz, t, d = x.shape
    ntok = bsz * t
    nb = t // TM
    h = x.reshape(ntok, d)
    tm_merge = TMM if (t % TMM == 0 or not grid_cols) and ntok % TMM == 0 else TM
    states = []
    depth = mods.shape[0]
    for l in range(depth):
        shift, scale, gate = mods[l, 0], mods[l, 1], mods[l, 2]
        q, k, v, bd, bdt, sza, osgu, re, im, szc, sg = _in_call(h, shift, scale, row_of_tile, nb, grid_cols, l, wts, consts)
        o_f, o_b, s_fin = _delta_call(q, k, v, bd, bdt, s0_all, min(l, s0_all.shape[1] - 1), t)
        states.append(s_fin)
        if grid_cols:
            shp = (bsz, t // GRID_W, GRID_W, W_BR)
            oc = _fourier_grid_call(consts["crk"], consts["srk"], re.reshape(shp), im.reshape(shp), szc.reshape(shp))
        else:
            shp = (bsz, t, W_BR)
            oc = _fourier_call(consts["cp"], consts["sp"], re.reshape(shp), im.reshape(shp), szc.reshape(shp), W_BR)
        h = _merge_call(h, o_f, o_b, sza, osgu, oc.reshape(ntok, W_BR), sg, gate, row_of_tile, l, wts,
                        final_g.reshape(1, d), l == depth - 1, tm_merge)
    return h.reshape(bsz, t, d), states


def kernel(x_prompt, x_sample, state_delta, c, c_ctx, ln_g, w_ada, b_ada, w_in, conv_w, a_log, dt_bias, o_norm_g,
           sgu_norm_g, w_spatial, b_spatial, w_branch, w_out, final_g):
    depth, d = ln_g.shape
    bp, tp, _ = x_prompt.shape
    bs, ts, _ = x_sample.shape
    assert conv_w.shape[1:] == (CONV_K, 3 * W_BR) and w_in.shape[2] == 9 * W_BR + 4 * H_A + 3 * d
    assert w_spatial.shape[1:] == (G_B, CHUNK_B, CHUNK_B) and state_delta.shape[2:] == (2, H_A, DK, DK)
    assert all(t % TM == 0 and (t % TD == 0 or TD % t == 0) for t in (tp, ts))
    assert TM % GRID_W == 0 and GRID_W % COLS_SLAB == 0
    consts = _consts(tp, ts // GRID_W)
    wts = _stacked_weights(w_in, ln_g, conv_w, a_log, dt_bias, o_norm_g, sgu_norm_g, w_spatial, b_spatial,
                           w_branch, w_out)
    nrow = -(-(1 + bs) // 8) * 8
    cvecs = jnp.concatenate([c_ctx[None], c, jnp.zeros((nrow - 1 - bs, d), F32)], axis=0)
    mods = _ada_call(cvecs, w_ada, b_ada).reshape(depth, nrow, 3, 1, d).transpose(0, 2, 1, 3, 4)

    zeros = jnp.zeros((bp, 1, 2, H_A, DK, DK), F32)
    y_prompt, st = _group_forward(x_prompt, mods, lambda i: 0, zeros, False, wts, consts, final_g)
    tiles_s = ts // TM
    y_sample, _ = _group_forward(x_sample, mods, lambda i: 1 + i // tiles_s, state_delta, True, wts, consts, final_g)
    return y_prompt, y_sample, jnp.stack(st, axis=1)
```
